```python
import math
import jax, jax.numpy as jnp
from jax import lax
import numpy as np

D_MODEL = 2048
BATCH = 4
SEQ = 8192
DEPTH = 1

HEAD_DIM = 128
MIX_W = D_MODEL
RET_HEADS = (MIX_W // 2) // HEAD_DIM
RET_W = RET_HEADS * HEAD_DIM
RET_CHUNK = 128
DSA_HEADS = (MIX_W // 2) // HEAD_DIM
DSA_KV_HEADS = 2
DSA_W = DSA_HEADS * HEAD_DIM
DSA_KV_W = DSA_KV_HEADS * HEAD_DIM
IDX_HEADS = 16
IDX_DIM = 64
TOPK_MAX = 256
Q_BLOCK = 128
D_FF = 4 * D_MODEL
N_MOD = 6
IN_SPLITS = (RET_W, RET_W, RET_W, RET_W,
             DSA_W, DSA_KV_W, DSA_KV_W,
             IDX_HEADS * IDX_DIM, IDX_DIM, IDX_HEADS)
IN_WIDTH = sum(IN_SPLITS)
EPS = 1e-6

kernel_name = "hybrid_retention_dsa_adaln_block"


def _rms_norm(x, g):
    xf = x.astype(jnp.float32)
    y = xf * lax.rsqrt(jnp.mean(xf * xf, axis=-1, keepdims=True) + EPS)
    return (y * g.astype(jnp.float32)).astype(x.dtype)


def _layer_norm_noaffine(x):
    xf = x.astype(jnp.float32)
    mu = jnp.mean(xf, axis=-1, keepdims=True)
    var = jnp.mean(jnp.square(xf - mu), axis=-1, keepdims=True)
    return ((xf - mu) * lax.rsqrt(var + EPS)).astype(x.dtype)


def _modulate(h, shift, scale):
    return h * (1.0 + scale[:, None, :]) + shift[:, None, :]


def _xpos_rotate(x, pos):
    d = x.shape[-1]
    angle = 1.0 / (10000.0 ** jnp.linspace(0.0, 1.0, d // 2, dtype=jnp.float32))
    theta = pos[:, None] * angle[None, :]
    sin = jnp.sin(theta)[None, :, None, :].astype(x.dtype)
    cos = jnp.cos(theta)[None, :, None, :].astype(x.dtype)
    x1, x2 = x[..., 0::2], x[..., 1::2]
    return jnp.stack([x1 * cos - x2 * sin, x1 * sin + x2 * cos], axis=-1).reshape(x.shape)


def _retention_chunkwise(q, k, v):
    B, L, H, D = q.shape
    C = RET_CHUNK
    N = L // C
    log_gamma = jnp.log(1.0 - jnp.exp2(-5.0 - jnp.arange(H, dtype=jnp.float32)))
    pos = jnp.arange(C, dtype=jnp.float32)
    diff = pos[:, None] - pos[None, :]
    dmask = jnp.where(diff[None] >= 0,
                      jnp.exp(log_gamma[:, None, None] * jnp.maximum(diff, 0.0)[None]), 0.0)
    xi = jnp.exp(log_gamma[:, None] * (pos[None, :] + 1.0))
    zeta = jnp.exp(log_gamma[:, None] * (C - 1.0 - pos[None, :]))
    chunk_decay = jnp.exp(log_gamma * C)

    def to_chunks(a):
        return a.reshape(B, N, C, H, D).transpose(1, 0, 3, 2, 4)

    def step(R, qkv):
        qc, kc, vc = qkv
        inner = jnp.einsum('bhcd,bhed->bhce', qc, kc) * dmask[None]
        o = (jnp.einsum('bhce,bhed->bhcd', inner, vc)
             + jnp.einsum('bhcd,bhde->bhce', qc, R) * xi[None, :, :, None])
        R = (R * chunk_decay[None, :, None, None]
             + jnp.einsum('bhcd,bhce->bhde', kc * zeta[None, :, :, None], vc))
        return R, o

    R0 = jnp.zeros((B, H, D, D), jnp.float32)
    _, o = lax.scan(step, R0, (to_chunks(q), to_chunks(k), to_chunks(v)))
    return o.transpose(1, 0, 3, 2, 4).reshape(B, L, H, D)


def _dsa_attention(q, k, v, q_idx, k_idx, w_idx):
    B, L, G, R, D = q.shape
    topk = min(TOPK_MAX, L // 4)
    n_blocks = L // Q_BLOCK
    key_pos = jnp.arange(L)

    def block(i):
        start = i * Q_BLOCK
        t = start + jnp.arange(Q_BLOCK)
        qb = lax.dynamic_slice_in_dim(q, start, Q_BLOCK, axis=1)
        qib = lax.dynamic_slice_in_dim(q_idx, start, Q_BLOCK, axis=1)
        wb = lax.dynamic_slice_in_dim(w_idx, start, Q_BLOCK, axis=1)
        s = jnp.einsum('bqhd,bsd->bqhs', qib, k_idx).astype(jnp.float32)
        score = jnp.einsum('bqhs,bqh->bqs', jax.nn.relu(s), wb.astype(jnp.float32))
        causal = key_pos[None, None, :] <= t[None, :, None]
        score = jnp.where(causal, score, -jnp.inf)
        _, idx = lax.top_k(score, topk)
        valid = idx <= t[None, :, None]
        k_sel = jax.vmap(lambda kb, ib: kb[ib])(k, idx)
        v_sel = jax.vmap(lambda vb, ib: vb[ib])(v, idx)
        logits = jnp.einsum('bqgrd,bqkgd->bqgrk', qb, k_sel).astype(jnp.float32)
        logits = jnp.where(valid[:, :, None, None, :], logits, -jnp.inf)
        p = jax.nn.softmax(logits, axis=-1).astype(v.dtype)
        o = jnp.einsum('bqgrk,bqkgd->bqgrd', p, v_sel)
        return o.reshape(B, Q_BLOCK, G * R * D)

    out = lax.map(block, jnp.arange(n_blocks))
    return out.transpose(1, 0, 2, 3).reshape(B, L, G * R * D)


def _layer(x, c, w_ada, b_ada, norm1_g, norm2_g, w_in, ret_beta, q_norm_g, k_norm_g,
           idx_k_norm_g, dsa_beta, w_out, w_mlp1, w_mlp2):
    B, L, _ = x.shape
    mod = jax.nn.silu(c) @ w_ada + b_ada
    shift1, scale1, gate1, shift2, scale2, gate2 = jnp.split(mod, N_MOD, axis=-1)

    h = _modulate(_rms_norm(x, norm1_g), shift1, scale1)
    proj = h @ w_in
    offs = list(np.cumsum(IN_SPLITS)[:-1])
    rq, rk, rv, rg, dq, dk, dv, iq, ik, iw = jnp.split(proj, offs, axis=-1)

    pos = jnp.arange(L, dtype=jnp.float32)
    rq = _xpos_rotate(rq.reshape(B, L, RET_HEADS, HEAD_DIM), pos)
    rk = _xpos_rotate(rk.reshape(B, L, RET_HEADS, HEAD_DIM), pos) * (HEAD_DIM ** -0.5)
    rv = rv.reshape(B, L, RET_HEADS, HEAD_DIM)
    ro = _retention_chunkwise(rq.astype(jnp.float32), rk.astype(jnp.float32), rv.astype(jnp.float32))
    ro = _layer_norm_noaffine(ro).astype(x.dtype).reshape(B, L, RET_W)
    y_ret = jax.nn.silu(rg) * (ro * ret_beta)

    dq = _rms_norm(dq.reshape(B, L, DSA_HEADS, HEAD_DIM), q_norm_g) * (HEAD_DIM ** -0.5)
    dq = dq.reshape(B, L, DSA_KV_HEADS, DSA_HEADS // DSA_KV_HEADS, HEAD_DIM)
    dk = _rms_norm(dk.reshape(B, L, DSA_KV_HEADS, HEAD_DIM), k_norm_g)
    dv = dv.reshape(B, L, DSA_KV_HEADS, HEAD_DIM)
    iq = iq.reshape(B, L, IDX_HEADS, IDX_DIM) * (IDX_DIM ** -0.5)
    ik = _rms_norm(ik, idx_k_norm_g)
    iw = iw * (IDX_HEADS ** -0.5)
    y_dsa = _dsa_attention(dq, dk, dv, iq, ik, iw) * dsa_beta

    mix = jnp.concatenate([y_ret, y_dsa], axis=-1) @ w_out
    x = x + gate1[:, None, :] * mix

    h = _modulate(_rms_norm(x, norm2_g), shift2, scale2)
    y = jnp.square(jax.nn.relu(h @ w_mlp1)) @ w_mlp2
    return x + gate2[:, None, :] * y


def setup_inputs(seed: int = 0) -> dict:
    key = jax.random.key(seed)
    ks = jax.random.split(key, 16)
    nrm = jax.random.normal
    f32 = jnp.float32
    return {
        "x": nrm(ks[0], (BATCH, SEQ, D_MODEL), f32),
        "c": nrm(ks[1], (BATCH, D_MODEL), f32),
        "w_ada": nrm(ks[2], (DEPTH, D_MODEL, N_MOD * D_MODEL), f32) * (0.5 * D_MODEL ** -0.5),
        "b_ada": nrm(ks[3], (DEPTH, N_MOD * D_MODEL), f32) * 0.01,
        "norm1_g": 1.0 + 0.02 * nrm(ks[4], (DEPTH, D_MODEL), f32),
        "norm2_g": 1.0 + 0.02 * nrm(ks[5], (DEPTH, D_MODEL), f32),
        "w_in": nrm(ks[6], (DEPTH, D_MODEL, IN_WIDTH), f32) * (D_MODEL ** -0.5),
        "ret_beta": 1.0 + 0.02 * nrm(ks[7], (DEPTH, RET_W), f32),
        "q_norm_g": 1.0 + 0.02 * nrm(ks[8], (DEPTH, HEAD_DIM), f32),
        "k_norm_g": 1.0 + 0.02 * nrm(ks[9], (DEPTH, HEAD_DIM), f32),
        "idx_k_norm_g": 1.0 + 0.02 * nrm(ks[10], (DEPTH, IDX_DIM), f32),
        "dsa_beta": 1.0 + 0.02 * nrm(ks[11], (DEPTH, DSA_W), f32),
        "w_out": nrm(ks[12], (DEPTH, MIX_W, D_MODEL), f32) * (MIX_W ** -0.5),
        "w_mlp1": nrm(ks[13], (DEPTH, D_MODEL, D_FF), f32) * (D_MODEL ** -0.5),
        "w_mlp2": nrm(ks[14], (DEPTH, D_FF, D_MODEL), f32) * (D_FF ** -0.5),
    }


def reference(x, c, w_ada, b_ada, norm1_g, norm2_g, w_in, ret_beta, q_norm_g, k_norm_g,
              idx_k_norm_g, dsa_beta, w_out, w_mlp1, w_mlp2):
    for l in range(DEPTH):
        x = _layer(x, c, w_ada[l], b_ada[l], norm1_g[l], norm2_g[l], w_in[l], ret_beta[l],
                   q_norm_g[l], k_norm_g[l], idx_k_norm_g[l], dsa_beta[l], w_out[l],
                   w_mlp1[l], w_mlp2[l])
    return x
```

```python
import functools
import math

import numpy as np
import jax
import jax.numpy as jnp
from jax import lax
from jax.experimental import pallas as pl
from jax.experimental.pallas import tpu as pltpu

F32 = jnp.float32
BF16 = jnp.bfloat16
I32 = jnp.int32

D_MODEL = 2048
HEAD_DIM = 128
RET_HEADS = 8
RET_W = 1024
RET_CHUNK = 128
DSA_HEADS = 8
DSA_KV_HEADS = 2
DSA_REP = DSA_HEADS // DSA_KV_HEADS
DSA_W = 1024
DSA_KV_W = 256
IDX_HEADS = 16
IDX_DIM = 64
TOPK_MAX = 256
Q_BLOCK = 128
D_FF = 4 * D_MODEL
N_MOD = 6
EPS = 1e-6

LANES = 128
KEY_CHUNK = 256
MLP_NSLICE = 512
VMEM_LIMIT = 56 * 1024 * 1024

P_TILE = 1024
P_NTILES = 7
P_WIDTH = P_TILE * P_NTILES
TAIL_COL = 6 * P_TILE

INT_MIN = -(2 ** 31)
NEG_BIG = -1e30
LOG2E = 1.4426950408889634


def _cparams(sem):
    return pltpu.CompilerParams(dimension_semantics=sem, vmem_limit_bytes=VMEM_LIMIT)


def _mod_kernel(c_ref, w_ref, b_ref, o_ref):
    c = c_ref[...]
    s = c * jax.nn.sigmoid(c)
    o_ref[...] = jnp.dot(s.astype(BF16), w_ref[...].astype(BF16),
                         preferred_element_type=F32) + b_ref[...]


def _mod_call(c, w_ada, b_ada):
    B = c.shape[0]
    n = w_ada.shape[1]
    tn = 1536
    return pl.pallas_call(
        _mod_kernel,
        grid=(n // tn,),
        in_specs=[pl.BlockSpec((B, D_MODEL), lambda j: (0, 0)),
                  pl.BlockSpec((D_MODEL, tn), lambda j: (0, j)),
                  pl.BlockSpec((1, tn), lambda j: (0, j))],
        out_specs=pl.BlockSpec((B, tn), lambda j: (0, j)),
        out_shape=jax.ShapeDtypeStruct((B, n), F32),
        compiler_params=_cparams(("arbitrary",)),
        name="mod",
    )(c, w_ada, b_ada.reshape(1, n))


def _rms_scale(xs, width):
    ms = jnp.sum(xs * xs, axis=-1, keepdims=True) * (1.0 / width)
    return xs * lax.rsqrt(ms + EPS)


def _inproj_kernel(x_ref, mod_ref, g1_ref, w_ref, cos_ref, sin_ref, qg_ref, kg_ref, ikg_ref,
                   p_ref, vt_ref, h_ref, *, tm):
    j = pl.program_id(1)

    @pl.when(j == 0)
    def _():
        x = x_ref[...]
        ms = jnp.mean(x * x, axis=-1, keepdims=True)
        y = (x * lax.rsqrt(ms + EPS)) * g1_ref[...]
        h = y * (1.0 + mod_ref[0, 1:2, :]) + mod_ref[0, 0:1, :]
        h_ref[...] = h.astype(BF16)

    acc = jnp.dot(h_ref[...], w_ref[...], preferred_element_type=F32)

    def rotate(scale):
        cos = cos_ref[...]
        sin = sin_ref[...]
        for hh in range(RET_HEADS):
            xs = acc[:, hh * LANES:(hh + 1) * LANES]
            r = xs * cos + pltpu.roll(xs, LANES // 2, 1) * sin
            if scale != 1.0:
                r = r * scale
            p_ref[:, hh * LANES:(hh + 1) * LANES] = r.astype(BF16)

    @pl.when(j == 0)
    def _():
        rotate(1.0)

    @pl.when(j == 1)
    def _():
        rotate(HEAD_DIM ** -0.5)

    @pl.when(j == 2)
    def _():
        p_ref[...] = acc.astype(BF16)

    @pl.when(j == 3)
    def _():
        p_ref[...] = (acc * jax.nn.sigmoid(acc)).astype(BF16)

    @pl.when(j == 4)
    def _():
        g = qg_ref[...] * (HEAD_DIM ** -0.5 * LOG2E)
        for hh in range(DSA_HEADS):
            xs = acc[:, hh * LANES:(hh + 1) * LANES]
            p_ref[:, hh * LANES:(hh + 1) * LANES] = (_rms_scale(xs, HEAD_DIM) * g).astype(BF16)

    @pl.when(j == 5)
    def _():
        p_ref[...] = (acc * (IDX_DIM ** -0.5)).astype(BF16)

    @pl.when(j == 6)
    def _():
        kg = kg_ref[...]
        for g in range(DSA_KV_HEADS):
            xs = acc[:, g * LANES:(g + 1) * LANES]
            p_ref[:, g * LANES:(g + 1) * LANES] = (_rms_scale(xs, HEAD_DIM) * kg).astype(BF16)
        dv = acc[:, 256:512]
        p_ref[:, 256:512] = dv.astype(BF16)
        for s in range(tm // KEY_CHUNK):
            blk = dv[s * KEY_CHUNK:(s + 1) * KEY_CHUNK, :]
            vt_ref[0, s] = blk.T.astype(BF16)
        ik = acc[:, 512:640]
        ika = _rms_scale(ik, IDX_DIM) * ikg_ref[...]
        p_ref[:, 512:640] = ika.astype(BF16)
        p_ref[:, 640:768] = pltpu.roll(ika, LANES // 2, 1).astype(BF16)
        p_ref[:, 768:896] = (acc[:, 768:896] * (IDX_HEADS ** -0.5)).astype(BF16)
        p_ref[:, 896:1024] = jnp.zeros((tm, LANES), BF16)


def _inproj_call(x2, mod3, g1, w_p, cos_t, sin_t, qg, kg, ikg, *, B, L, tm):
    T = B * L
    tpb = L // tm
    kern = functools.partial(_inproj_kernel, tm=tm)
    return pl.pallas_call(
        kern,
        grid=(T // tm, P_NTILES),
        in_specs=[
            pl.BlockSpec((tm, D_MODEL), lambda i, j: (i, 0)),
            pl.BlockSpec((1, N_MOD, D_MODEL), lambda i, j: (i // tpb, 0, 0)),
            pl.BlockSpec((1, D_MODEL), lambda i, j: (0, 0)),
            pl.BlockSpec((D_MODEL, P_TILE), lambda i, j: (0, j)),
            pl.BlockSpec((tm, LANES), lambda i, j: (i % tpb, 0)),
            pl.BlockSpec((tm, LANES), lambda i, j: (i % tpb, 0)),
            pl.BlockSpec((1, LANES), lambda i, j: (0, 0)),
            pl.BlockSpec((1, LANES), lambda i, j: (0, 0)),
            pl.BlockSpec((1, LANES), lambda i, j: (0, 0)),
        ],
        out_specs=[
            pl.BlockSpec((tm, P_TILE), lambda i, j: (i, j)),
            pl.BlockSpec((1, tm // KEY_CHUNK, KEY_CHUNK, KEY_CHUNK),
                         lambda i, j: (i // tpb, i % tpb, 0, 0)),
        ],
        out_shape=[jax.ShapeDtypeStruct((T, P_WIDTH), BF16),
                   jax.ShapeDtypeStruct((B, L // KEY_CHUNK, KEY_CHUNK, KEY_CHUNK), BF16)],
        scratch_shapes=[pltpu.VMEM((tm, D_MODEL), BF16)],
        compiler_params=_cparams(("arbitrary", "arbitrary")),
        name="inproj",
    )(x2, mod3, g1, w_p, cos_t, sin_t, qg, kg, ikg)


def _ret_consts():
    h = np.arange(RET_HEADS, dtype=np.float32)
    log_gamma = np.log(np.float32(1.0) - np.exp2(np.float32(-5.0) - h)).astype(np.float32)
    return log_gamma


def _retention_kernel(q_ref, k_ref, v_ref, g_ref, dm_ref, xi_ref, zeta_ref, beta_ref,
                      o_ref, r_ref, *, rt, decay):
    @pl.when(pl.program_id(1) == 0)
    def _():
        r_ref[...] = jnp.zeros(r_ref.shape, F32)

    nt = (((1,), (1,)), ((), ()))

    def chunk(c, carry):
        r0 = pl.multiple_of(c * RET_CHUNK, RET_CHUNK)
        for hh in range(RET_HEADS):
            cs = slice(hh * LANES, (hh + 1) * LANES)
            q = q_ref[pl.ds(r0, RET_CHUNK), cs]
            k = k_ref[pl.ds(r0, RET_CHUNK), cs]
            v = v_ref[pl.ds(r0, RET_CHUNK), cs]
            inner = lax.dot_general(q, k, nt, preferred_element_type=F32) * dm_ref[hh]
            state = r_ref[hh]
            o = (jnp.dot(inner.astype(BF16), v, preferred_element_type=F32)
                 + jnp.dot(q, state.astype(BF16), preferred_element_type=F32) * xi_ref[hh])
            kz = k.astype(F32) * zeta_ref[hh]
            r_ref[hh] = state * decay[hh] + jnp.dot(kz.T.astype(BF16), v,
                                                    preferred_element_type=F32)
            mu = jnp.mean(o, axis=-1, keepdims=True)
            d = o - mu
            var = jnp.mean(d * d, axis=-1, keepdims=True)
            on = d * lax.rsqrt(var + EPS)
            y = g_ref[pl.ds(r0, RET_CHUNK), cs].astype(F32) * (on * beta_ref[:, cs])
            o_ref[pl.ds(r0, RET_CHUNK), cs] = y.astype(BF16)
        return carry

    lax.fori_loop(0, rt // RET_CHUNK, chunk, 0)


def _retention_call(P, dmask, xi_b, zeta_b, beta, decay, *, B, L, rt):
    T = B * L
    nb = L // rt
    kern = functools.partial(_retention_kernel, rt=rt, decay=decay)
    blk = lambda col: pl.BlockSpec((rt, P_TILE), lambda b, l, col=col: (b * nb + l, col))
    cst = pl.BlockSpec((RET_HEADS, RET_CHUNK, RET_CHUNK), lambda b, l: (0, 0, 0))
    return pl.pallas_call(
        kern,
        grid=(B, nb),
        in_specs=[blk(0), blk(1), blk(2), blk(3), cst, cst, cst,
                  pl.BlockSpec((1, RET_W), lambda b, l: (0, 0))],
        out_specs=pl.BlockSpec((rt, RET_W), lambda b, l: (b * nb + l, 0)),
        out_shape=jax.ShapeDtypeStruct((T, RET_W), BF16),
        scratch_shapes=[pltpu.VMEM((RET_HEADS, HEAD_DIM, HEAD_DIM), F32)],
        compiler_params=_cparams(("arbitrary", "arbitrary")),
        name="retention",
    )(P, P, P, P, dmask, xi_b, zeta_b, beta)


def _sortable_key(x):
    b = pltpu.bitcast(x, I32)
    return b ^ ((b >> 31) & 0x7FFFFFFF)


def _dsa_kernel(dq_ref, iq_ref, iw_ref, dk_ref, vt_ref, ika_ref, ikb_ref, beta_ref, o_ref,
                s_ref, lt_ref, qt_ref, acc_ref, m_ref, l_ref, *, topk):
    i = pl.program_id(1)
    n128 = i + 1
    n256 = (i + 2) // 2
    q0 = i * Q_BLOCK

    iq = iq_ref[...].astype(F32)
    for p in range(IDX_HEADS // 2):
        lt_ref[:, p * LANES:(p + 1) * LANES] = iq[:, p * LANES:(p + 1) * LANES].T.astype(BF16)
    dq = dq_ref[...].astype(F32)
    for hh in range(DSA_HEADS):
        qt_ref[:, hh * LANES:(hh + 1) * LANES] = dq[:, hh * LANES:(hh + 1) * LANES].T.astype(BF16)
    iwt = iw_ref[...].astype(F32).T
    w_even = jnp.concatenate([iwt[2 * p:2 * p + 1, :] for p in range(IDX_HEADS // 2)], axis=1)
    w_odd = jnp.concatenate([iwt[2 * p + 1:2 * p + 2, :] for p in range(IDX_HEADS // 2)], axis=1)

    qpos = q0 + lax.broadcasted_iota(I32, (KEY_CHUNK, Q_BLOCK), 1)
    krow = lax.broadcasted_iota(I32, (KEY_CHUNK, Q_BLOCK), 0)

    def score_chunk(c, masked):
        k0 = pl.multiple_of(c * KEY_CHUNK, KEY_CHUNK)
        lt = lt_ref[...]
        se = jnp.dot(ika_ref[pl.ds(k0, KEY_CHUNK), :], lt, preferred_element_type=F32)
        so = jnp.dot(ikb_ref[pl.ds(k0, KEY_CHUNK), :], lt, preferred_element_type=F32)
        te = jnp.maximum(se, 0.0) * w_even
        to = jnp.maximum(so, 0.0) * w_odd
        sc = jnp.zeros((KEY_CHUNK, Q_BLOCK), F32)
        for p in range(IDX_HEADS // 2):
            sc = sc + te[:, p * LANES:(p + 1) * LANES]
            sc = sc + to[:, p * LANES:(p + 1) * LANES]
        key = _sortable_key(sc)
        if masked:
            key = jnp.where(krow + k0 <= qpos, key, INT_MIN)
        s_ref[pl.ds(k0, KEY_CHUNK), :] = key

    def score_body(c, carry):
        score_chunk(c, False)
        return carry

    lax.fori_loop(0, n256 - 1, score_body, 0)
    score_chunk(n256 - 1, True)

    def search_pass(b, t_u):
        bit = 31 - b
        cand_u = t_u | jnp.left_shift(jnp.int32(1), bit)
        cand_s = cand_u ^ INT_MIN

        def count_chunk(c, cnt):
            k0 = pl.multiple_of(c * Q_BLOCK, Q_BLOCK)
            blk = s_ref[pl.ds(k0, Q_BLOCK), :].reshape(Q_BLOCK // 8, 8, Q_BLOCK)
            ge = jnp.where(blk >= cand_s[None], 1, 0)
            return cnt + jnp.sum(ge, axis=0)

        cnt = lax.fori_loop(0, n128, count_chunk, jnp.zeros((8, Q_BLOCK), I32))
        tot = jnp.sum(cnt, axis=0, keepdims=True)
        return jnp.where(tot >= topk, cand_u, t_u)

    t_u = lax.fori_loop(0, 32, search_pass, jnp.zeros((8, Q_BLOCK), I32))
    thr = jnp.maximum(t_u[0:1, :] ^ INT_MIN, INT_MIN + 1)

    acc_ref[...] = jnp.zeros(acc_ref.shape, F32)
    m_ref[...] = jnp.full(m_ref.shape, NEG_BIG, F32)
    l_ref[...] = jnp.zeros(l_ref.shape, F32)

    def attn_chunk(c, carry):
        k0 = pl.multiple_of(c * KEY_CHUNK, KEY_CHUNK)
        sel = s_ref[pl.ds(k0, KEY_CHUNK), :] >= thr
        for g in range(DSA_KV_HEADS):
            kch = dk_ref[pl.ds(k0, KEY_CHUNK), g * LANES:(g + 1) * LANES]
            qt = qt_ref[:, g * DSA_REP * LANES:(g + 1) * DSA_REP * LANES]
            st = jnp.dot(kch, qt, preferred_element_type=F32)
            st = jnp.concatenate(
                [jnp.where(sel, st[:, r * LANES:(r + 1) * LANES], NEG_BIG) for r in range(DSA_REP)],
                axis=1)
            m_old = m_ref[g:g + 1, :]
            m_new = jnp.maximum(m_old, jnp.max(st, axis=0, keepdims=True))
            alpha = jnp.exp2(m_old - m_new)
            p = jnp.exp2(st - m_new)
            l_ref[g:g + 1, :] = alpha * l_ref[g:g + 1, :] + jnp.sum(p, axis=0, keepdims=True)
            m_ref[g:g + 1, :] = m_new
            vt = vt_ref[0, c, g * LANES:(g + 1) * LANES, :]
            pv = jnp.dot(vt, p.astype(BF16), preferred_element_type=F32)
            acc_ref[g] = acc_ref[g] * alpha + pv
        return carry

    lax.fori_loop(0, n256, attn_chunk, 0)

    beta = beta_ref[...]
    for g in range(DSA_KV_HEADS):
        ot = acc_ref[g] / l_ref[g:g + 1, :]
        for r in range(DSA_REP):
            hh = g * DSA_REP + r
            blk = ot[:, r * LANES:(r + 1) * LANES].T
            o_ref[:, hh * LANES:(hh + 1) * LANES] = (
                blk * beta[:, hh * LANES:(hh + 1) * LANES]).astype(BF16)


def _dsa_call(P, vT, beta, *, B, L, topk):
    T = B * L
    nq = L // Q_BLOCK
    kern = functools.partial(_dsa_kernel, topk=topk)
    qrow = lambda col, width: pl.BlockSpec((Q_BLOCK, width),
                                           lambda b, i, col=col: (b * nq + i, col))
    tail128 = TAIL_COL // LANES
    return pl.pallas_call(
        kern,
        grid=(B, nq),
        in_specs=[
            qrow(4, P_TILE),
            qrow(5, P_TILE),
            qrow(tail128 + 6, LANES),
            pl.BlockSpec((L, DSA_KV_W), lambda b, i: (b, TAIL_COL // DSA_KV_W)),
            pl.BlockSpec((1, L // KEY_CHUNK, KEY_CHUNK, KEY_CHUNK), lambda b, i: (b, 0, 0, 0)),
            pl.BlockSpec((L, LANES), lambda b, i: (b, tail128 + 4)),
            pl.BlockSpec((L, LANES), lambda b, i: (b, tail128 + 5)),
            pl.BlockSpec((1, DSA_W), lambda b, i: (0, 0)),
        ],
        out_specs=pl.BlockSpec((Q_BLOCK, DSA_W), lambda b, i: (b * nq + i, 0)),
        out_shape=jax.ShapeDtypeStruct((T, DSA_W), BF16),
        scratch_shapes=[
            pltpu.VMEM((L, Q_BLOCK), I32),
            pltpu.VMEM((LANES, P_TILE), BF16),
            pltpu.VMEM((HEAD_DIM, DSA_W), BF16),
            pltpu.VMEM((DSA_KV_HEADS, HEAD_DIM, DSA_REP * Q_BLOCK), F32),
            pltpu.VMEM((DSA_KV_HEADS, DSA_REP * Q_BLOCK), F32),
            pltpu.VMEM((DSA_KV_HEADS, DSA_REP * Q_BLOCK), F32),
        ],
        compiler_params=_cparams(("arbitrary", "arbitrary")),
        name="dsa",
    )(P, P, P, P, vT, P, P, beta)


def _outproj_kernel(yr_ref, yd_ref, x_ref, mod_ref, w_ref, o_ref):
    mix = (jnp.dot(yr_ref[...], w_ref[0:RET_W, :], preferred_element_type=F32)
           + jnp.dot(yd_ref[...], w_ref[RET_W:RET_W + DSA_W, :], preferred_element_type=F32))
    o_ref[...] = x_ref[...] + mod_ref[0, 2:3, :] * mix


def _outproj_call(y_ret, y_dsa, x2, mod3, w_out, *, B, L, tm):
    T = B * L
    tpb = L // tm
    return pl.pallas_call(
        _outproj_kernel,
        grid=(T // tm,),
        in_specs=[
            pl.BlockSpec((tm, RET_W), lambda i: (i, 0)),
            pl.BlockSpec((tm, DSA_W), lambda i: (i, 0)),
            pl.BlockSpec((tm, D_MODEL), lambda i: (i, 0)),
            pl.BlockSpec((1, N_MOD, D_MODEL), lambda i: (i // tpb, 0, 0)),
            pl.BlockSpec((RET_W + DSA_W, D_MODEL), lambda i: (0, 0)),
        ],
        out_specs=pl.BlockSpec((tm, D_MODEL), lambda i: (i, 0)),
        out_shape=jax.ShapeDtypeStruct((T, D_MODEL), F32),
        compiler_params=_cparams(("arbitrary",)),
        name="outproj",
    )(y_ret, y_dsa, x2, mod3, w_out)


def _mlp_kernel(x_ref, mod_ref, g2_ref, w1_ref, w2_ref, o_ref, h_ref, *, nf):
    f = pl.program_id(1)

    @pl.when(f == 0)
    def _():
        x = x_ref[...]
        ms = jnp.mean(x * x, axis=-1, keepdims=True)
        y = (x * lax.rsqrt(ms + EPS)) * g2_ref[...]
        h = y * (1.0 + mod_ref[0, 4:5, :]) + mod_ref[0, 3:4, :]
        h_ref[...] = h.astype(BF16)

    hid = jnp.dot(h_ref[...], w1_ref[...], preferred_element_type=F32)
    a = jnp.maximum(hid, 0.0)
    a = (a * a).astype(BF16)
    for n in range(D_MODEL // MLP_NSLICE):
        cs = slice(n * MLP_NSLICE, (n + 1) * MLP_NSLICE)
        contrib = jnp.dot(a, w2_ref[:, cs], preferred_element_type=F32)

        @pl.when(f == 0)
        def _():
            o_ref[:, cs] = contrib

        @pl.when(jnp.logical_and(f > 0, f < nf - 1))
        def _():
            o_ref[:, cs] += contrib

        @pl.when(f == nf - 1)
        def _():
            o_ref[:, cs] = x_ref[:, cs] + mod_ref[0, 5:6, cs] * (o_ref[:, cs] + contrib)


def _mlp_call(x1, mod3, g2, w1, w2, *, B, L, tm, tf):
    T = B * L
    tpb = L // tm
    nf = D_FF // tf
    kern = functools.partial(_mlp_kernel, nf=nf)
    return pl.pallas_call(
        kern,
        grid=(T // tm, nf),
        in_specs=[
            pl.BlockSpec((tm, D_MODEL), lambda i, f: (i, 0)),
            pl.BlockSpec((1, N_MOD, D_MODEL), lambda i, f: (i // tpb, 0, 0)),
            pl.BlockSpec((1, D_MODEL), lambda i, f: (0, 0)),
            pl.BlockSpec((D_MODEL, tf), lambda i, f: (0, f)),
            pl.BlockSpec((tf, D_MODEL), lambda i, f: (f, 0)),
        ],
        out_specs=pl.BlockSpec((tm, D_MODEL), lambda i, f: (i, 0)),
        out_shape=jax.ShapeDtypeStruct((T, D_MODEL), F32),
        scratch_shapes=[pltpu.VMEM((tm, D_MODEL), BF16)],
        compiler_params=_cparams(("arbitrary", "arbitrary")),
        name="mlp",
    )(x1, mod3, g2, w1, w2)


def _deinterleave_heads(w):
    k = w.shape[0]
    return w.reshape(k, RET_HEADS, HEAD_DIM // 2, 2).transpose(0, 1, 3, 2).reshape(k, RET_W)


def _prep_w_in(w):
    o = np.cumsum([0, RET_W, RET_W, RET_W, RET_W, DSA_W, DSA_KV_W, DSA_KV_W,
                   IDX_HEADS * IDX_DIM, IDX_DIM, IDX_HEADS])
    part = lambda n: w[:, o[n]:o[n + 1]]
    z = lambda n: jnp.zeros((w.shape[0], n), w.dtype)
    cols = [_deinterleave_heads(part(0)), _deinterleave_heads(part(1)), part(2), part(3),
            part(4), part(7),
            part(5), part(6), part(8), z(LANES - IDX_DIM), z(LANES),
            part(9), z(LANES - IDX_HEADS), z(LANES)]
    return jnp.concatenate(cols, axis=1).astype(BF16)


def _rot_tables(L):
    pos = jnp.arange(L, dtype=F32)
    angle = 1.0 / (10000.0 ** jnp.linspace(0.0, 1.0, HEAD_DIM // 2, dtype=F32))
    theta = pos[:, None] * angle[None, :]
    s, c = jnp.sin(theta), jnp.cos(theta)
    return jnp.concatenate([c, c], axis=1), jnp.concatenate([-s, s], axis=1)


def _ret_tables():
    log_gamma = jnp.log(1.0 - jnp.exp2(-5.0 - jnp.arange(RET_HEADS, dtype=F32)))
    C = RET_CHUNK
    pos = jnp.arange(C, dtype=F32)
    diff = pos[:, None] - pos[None, :]
    dmask = jnp.where(diff[None] >= 0,
                      jnp.exp(log_gamma[:, None, None] * jnp.maximum(diff, 0.0)[None]), 0.0)
    xi = jnp.exp(log_gamma[:, None] * (pos[None, :] + 1.0))
    zeta = jnp.exp(log_gamma[:, None] * (C - 1.0 - pos[None, :]))
    bc = lambda a: jnp.broadcast_to(a[:, :, None], (RET_HEADS, C, C))
    return dmask, bc(xi), bc(zeta)


def _pad_lanes(v, n):
    return jnp.concatenate([v, jnp.zeros((n - v.shape[0],), v.dtype)]).reshape(1, n)


def _layer(x, c, w_ada, b_ada, norm1_g, norm2_g, w_in, ret_beta, q_norm_g, k_norm_g,
           idx_k_norm_g, dsa_beta, w_out, w_mlp1, w_mlp2):
    B, L, _ = x.shape
    topk = min(TOPK_MAX, L // 4)
    tm = min(1024, L)
    x2 = x.reshape(B * L, D_MODEL)

    mod3 = _mod_call(c, w_ada, b_ada).reshape(B, N_MOD, D_MODEL)

    cos_t, sin_t = _rot_tables(L)
    P, vT = _inproj_call(x2, mod3, norm1_g.reshape(1, -1), _prep_w_in(w_in), cos_t, sin_t,
                         q_norm_g.reshape(1, -1), k_norm_g.reshape(1, -1),
                         _pad_lanes(idx_k_norm_g, LANES), B=B, L=L, tm=tm)

    dmask, xi_b, zeta_b = _ret_tables()
    decay = tuple(float(v) for v in np.exp(_ret_consts() * np.float32(RET_CHUNK)))
    y_ret = _retention_call(P, dmask, xi_b, zeta_b, ret_beta.reshape(1, -1), decay,
                            B=B, L=L, rt=tm)
    y_dsa = _dsa_call(P, vT, dsa_beta.reshape(1, -1), B=B, L=L, topk=topk)

    x1 = _outproj_call(y_ret, y_dsa, x2, mod3, w_out.astype(BF16), B=B, L=L, tm=min(512, L))
    out = _mlp_call(x1, mod3, norm2_g.reshape(1, -1), w_mlp1.astype(BF16), w_mlp2.astype(BF16),
                    B=B, L=L, tm=tm, tf=512)
    return out.reshape(B, L, D_MODEL)


def kernel(x, c, w_ada, b_ada, norm1_g, norm2_g, w_in, ret_beta, q_norm_g, k_norm_g,
           idx_k_norm_g, dsa_beta, w_out, w_mlp1, w_mlp2):
    for l in range(w_ada.shape[0]):
        x = _layer(x, c, w_ada[l], b_ada[l], norm1_g[l], norm2_g[l], w_in[l], ret_beta[l],
                   q_norm_g[l], k_norm_g[l], idx_k_norm_g[l], dsa_beta[l], w_out[l],
                   w_mlp1[l], w_mlp2[l])
    return x
```

```python
import functools
import math

import numpy as np
import jax
import jax.numpy as jnp
from jax import lax
from jax.experimental import pallas as pl
from jax.experimental.pallas import tpu as pltpu

F32 = jnp.float32
BF16 = jnp.bfloat16
I32 = jnp.int32

D_MODEL = 2048
HEAD_DIM = 128
RET_HEADS = 8
RET_W = 1024
RET_CHUNK = 128
DSA_HEADS = 8
DSA_KV_HEADS = 2
DSA_REP = DSA_HEADS // DSA_KV_HEADS
DSA_W = 1024
DSA_KV_W = 256
IDX_HEADS = 16
IDX_DIM = 64
TOPK_MAX = 256
Q_BLOCK = 128
D_FF = 4 * D_MODEL
N_MOD = 6
EPS = 1e-6

LANES = 128
KEY_CHUNK = 256
MLP_NSLICE = 512
VMEM_LIMIT = 56 * 1024 * 1024

P_TILE = 1024
P_NTILES = 7
P_WIDTH = P_TILE * P_NTILES
TAIL_COL = 6 * P_TILE

I16 = jnp.int16
SEARCH_ROWS = 512
INT_MIN = -(2 ** 31)
MIN16 = -(2 ** 15)
NEG_BIG = -1e30
LOG2E = 1.4426950408889634


def _cparams(sem):
    return pltpu.CompilerParams(dimension_semantics=sem, vmem_limit_bytes=VMEM_LIMIT)


def _mod_kernel(c_ref, w_ref, b_ref, o_ref):
    c = c_ref[...]
    s = c * jax.nn.sigmoid(c)
    o_ref[...] = jnp.dot(s.astype(BF16), w_ref[...].astype(BF16),
                         preferred_element_type=F32) + b_ref[...]


def _mod_call(c, w_ada, b_ada):
    B = c.shape[0]
    n = w_ada.shape[1]
    tn = 1536
    return pl.pallas_call(
        _mod_kernel,
        grid=(n // tn,),
        in_specs=[pl.BlockSpec((B, D_MODEL), lambda j: (0, 0)),
                  pl.BlockSpec((D_MODEL, tn), lambda j: (0, j)),
                  pl.BlockSpec((1, tn), lambda j: (0, j))],
        out_specs=pl.BlockSpec((B, tn), lambda j: (0, j)),
        out_shape=jax.ShapeDtypeStruct((B, n), F32),
        compiler_params=_cparams(("arbitrary",)),
        name="mod",
    )(c, w_ada, b_ada.reshape(1, n))


def _rms_scale(xs, width):
    ms = jnp.sum(xs * xs, axis=-1, keepdims=True) * (1.0 / width)
    return xs * lax.rsqrt(ms + EPS)


def _inproj_kernel(x_ref, mod_ref, g1_ref, w_ref, cos_ref, sin_ref, qg_ref, kg_ref, ikg_ref,
                   p_ref, vt_ref, h_ref, *, tm):
    j = pl.program_id(1)

    @pl.when(j == 0)
    def _():
        x = x_ref[...]
        ms = jnp.mean(x * x, axis=-1, keepdims=True)
        y = (x * lax.rsqrt(ms + EPS)) * g1_ref[...]
        h = y * (1.0 + mod_ref[0, 1:2, :]) + mod_ref[0, 0:1, :]
        h_ref[...] = h.astype(BF16)

    acc = jnp.dot(h_ref[...], w_ref[...], preferred_element_type=F32)

    def rotate(scale):
        cos = cos_ref[...]
        sin = sin_ref[...]
        for hh in range(RET_HEADS):
            xs = acc[:, hh * LANES:(hh + 1) * LANES]
            r = xs * cos + pltpu.roll(xs, LANES // 2, 1) * sin
            if scale != 1.0:
                r = r * scale
            p_ref[:, hh * LANES:(hh + 1) * LANES] = r.astype(BF16)

    @pl.when(j == 0)
    def _():
        rotate(1.0)

    @pl.when(j == 1)
    def _():
        rotate(HEAD_DIM ** -0.5)

    @pl.when(j == 2)
    def _():
        p_ref[...] = acc.astype(BF16)

    @pl.when(j == 3)
    def _():
        p_ref[...] = (acc * jax.nn.sigmoid(acc)).astype(BF16)

    @pl.when(j == 4)
    def _():
        g = qg_ref[...] * (HEAD_DIM ** -0.5 * LOG2E)
        for hh in range(DSA_HEADS):
            xs = acc[:, hh * LANES:(hh + 1) * LANES]
            p_ref[:, hh * LANES:(hh + 1) * LANES] = (_rms_scale(xs, HEAD_DIM) * g).astype(BF16)

    @pl.when(j == 5)
    def _():
        p_ref[...] = (acc * (IDX_DIM ** -0.5)).astype(BF16)

    @pl.when(j == 6)
    def _():
        kg = kg_ref[...]
        for g in range(DSA_KV_HEADS):
            xs = acc[:, g * LANES:(g + 1) * LANES]
            p_ref[:, g * LANES:(g + 1) * LANES] = (_rms_scale(xs, HEAD_DIM) * kg).astype(BF16)
        dv = acc[:, 256:512]
        p_ref[:, 256:512] = dv.astype(BF16)
        for s in range(tm // KEY_CHUNK):
            blk = dv[s * KEY_CHUNK:(s + 1) * KEY_CHUNK, :]
            vt_ref[0, s] = blk.T.astype(BF16)
        ik = acc[:, 512:640]
        ika = _rms_scale(ik, IDX_DIM) * ikg_ref[...]
        p_ref[:, 512:640] = ika.astype(BF16)
        p_ref[:, 640:768] = pltpu.roll(ika, LANES // 2, 1).astype(BF16)
        p_ref[:, 768:896] = (acc[:, 768:896] * (IDX_HEADS ** -0.5)).astype(BF16)
        p_ref[:, 896:1024] = jnp.zeros((tm, LANES), BF16)


def _inproj_call(x2, mod3, g1, w_p, cos_t, sin_t, qg, kg, ikg, *, B, L, tm):
    T = B * L
    tpb = L // tm
    kern = functools.partial(_inproj_kernel, tm=tm)
    return pl.pallas_call(
        kern,
        grid=(T // tm, P_NTILES),
        in_specs=[
            pl.BlockSpec((tm, D_MODEL), lambda i, j: (i, 0)),
            pl.BlockSpec((1, N_MOD, D_MODEL), lambda i, j: (i // tpb, 0, 0)),
            pl.BlockSpec((1, D_MODEL), lambda i, j: (0, 0)),
            pl.BlockSpec((D_MODEL, P_TILE), lambda i, j: (0, j)),
            pl.BlockSpec((tm, LANES), lambda i, j: (i % tpb, 0)),
            pl.BlockSpec((tm, LANES), lambda i, j: (i % tpb, 0)),
            pl.BlockSpec((1, LANES), lambda i, j: (0, 0)),
            pl.BlockSpec((1, LANES), lambda i, j: (0, 0)),
            pl.BlockSpec((1, LANES), lambda i, j: (0, 0)),
        ],
        out_specs=[
            pl.BlockSpec((tm, P_TILE), lambda i, j: (i, j)),
            pl.BlockSpec((1, tm // KEY_CHUNK, KEY_CHUNK, KEY_CHUNK),
                         lambda i, j: (i // tpb, i % tpb, 0, 0)),
        ],
        out_shape=[jax.ShapeDtypeStruct((T, P_WIDTH), BF16),
                   jax.ShapeDtypeStruct((B, L // KEY_CHUNK, KEY_CHUNK, KEY_CHUNK), BF16)],
        scratch_shapes=[pltpu.VMEM((tm, D_MODEL), BF16)],
        compiler_params=_cparams(("arbitrary", "arbitrary")),
        name="inproj",
    )(x2, mod3, g1, w_p, cos_t, sin_t, qg, kg, ikg)


def _ret_consts():
    h = np.arange(RET_HEADS, dtype=np.float32)
    log_gamma = np.log(np.float32(1.0) - np.exp2(np.float32(-5.0) - h)).astype(np.float32)
    return log_gamma


def _retention_kernel(q_ref, k_ref, v_ref, g_ref, dm_ref, xi_ref, zeta_ref, beta_ref,
                      o_ref, r_ref, *, rt, decay):
    @pl.when(pl.program_id(1) == 0)
    def _():
        r_ref[...] = jnp.zeros(r_ref.shape, F32)

    nt = (((1,), (1,)), ((), ()))

    def chunk(c, carry):
        r0 = pl.multiple_of(c * RET_CHUNK, RET_CHUNK)
        for hh in range(RET_HEADS):
            cs = slice(hh * LANES, (hh + 1) * LANES)
            q = q_ref[pl.ds(r0, RET_CHUNK), cs]
            k = k_ref[pl.ds(r0, RET_CHUNK), cs]
            v = v_ref[pl.ds(r0, RET_CHUNK), cs]
            inner = lax.dot_general(q, k, nt, preferred_element_type=F32) * dm_ref[hh]
            state = r_ref[hh]
            o = (jnp.dot(inner.astype(BF16), v, preferred_element_type=F32)
                 + jnp.dot(q, state.astype(BF16), preferred_element_type=F32) * xi_ref[hh])
            kz = k.astype(F32) * zeta_ref[hh]
            r_ref[hh] = state * decay[hh] + jnp.dot(kz.T.astype(BF16), v,
                                                    preferred_element_type=F32)
            mu = jnp.mean(o, axis=-1, keepdims=True)
            d = o - mu
            var = jnp.mean(d * d, axis=-1, keepdims=True)
            on = d * lax.rsqrt(var + EPS)
            y = g_ref[pl.ds(r0, RET_CHUNK), cs].astype(F32) * (on * beta_ref[:, cs])
            o_ref[pl.ds(r0, RET_CHUNK), cs] = y.astype(BF16)
        return carry

    lax.fori_loop(0, rt // RET_CHUNK, chunk, 0)


def _retention_call(P, dmask, xi_b, zeta_b, beta, decay, *, B, L, rt):
    T = B * L
    nb = L // rt
    kern = functools.partial(_retention_kernel, rt=rt, decay=decay)
    blk = lambda col: pl.BlockSpec((rt, P_TILE), lambda b, l, col=col: (b * nb + l, col))
    cst = pl.BlockSpec((RET_HEADS, RET_CHUNK, RET_CHUNK), lambda b, l: (0, 0, 0))
    return pl.pallas_call(
        kern,
        grid=(B, nb),
        in_specs=[blk(0), blk(1), blk(2), blk(3), cst, cst, cst,
                  pl.BlockSpec((1, RET_W), lambda b, l: (0, 0))],
        out_specs=pl.BlockSpec((rt, RET_W), lambda b, l: (b * nb + l, 0)),
        out_shape=jax.ShapeDtypeStruct((T, RET_W), BF16),
        scratch_shapes=[pltpu.VMEM((RET_HEADS, HEAD_DIM, HEAD_DIM), F32)],
        compiler_params=_cparams(("arbitrary", "arbitrary")),
        name="retention",
    )(P, P, P, P, dmask, xi_b, zeta_b, beta)


def _sortable_key(x):
    b = pltpu.bitcast(x, I32)
    return b ^ ((b >> 31) & 0x7FFFFFFF)


def _tree_sum(parts):
    parts = list(parts)
    while len(parts) > 1:
        nxt = [parts[k] + parts[k + 1] for k in range(0, len(parts) - 1, 2)]
        if len(parts) % 2:
            nxt.append(parts[-1])
        parts = nxt
    return parts[0]


def _dsa_kernel(dq_ref, iq_ref, iw_ref, dk_ref, vt_ref, ika_ref, ikb_ref, beta_ref, o_ref,
                hi_ref, lo_ref, lo2_ref, sta_ref, stb_ref, lt_ref, qt_ref, acc_ref, m_ref, l_ref,
                *, topk):
    i = pl.program_id(1)
    n256 = (i + 2) // 2
    n512 = (n256 + 1) // 2
    q0 = i * Q_BLOCK
    pk = SEARCH_ROWS // 16
    min16 = jnp.int16(MIN16)
    one16 = jnp.int16(1)
    zero16 = jnp.int16(0)

    iq = iq_ref[...].astype(F32)
    for p in range(IDX_HEADS // 2):
        lt_ref[:, p * LANES:(p + 1) * LANES] = iq[:, p * LANES:(p + 1) * LANES].T.astype(BF16)
    dq = dq_ref[...].astype(F32)
    for hh in range(DSA_HEADS):
        qt_ref[:, hh * LANES:(hh + 1) * LANES] = dq[:, hh * LANES:(hh + 1) * LANES].T.astype(BF16)
    iwt = iw_ref[...].astype(F32).T
    w_even = jnp.concatenate([iwt[2 * p:2 * p + 1, :] for p in range(IDX_HEADS // 2)], axis=1)
    w_odd = jnp.concatenate([iwt[2 * p + 1:2 * p + 2, :] for p in range(IDX_HEADS // 2)], axis=1)

    qpos = q0 + lax.broadcasted_iota(I32, (KEY_CHUNK, Q_BLOCK), 1)
    krow = lax.broadcasted_iota(I32, (KEY_CHUNK, Q_BLOCK), 0)

    def score_chunk(c, masked):
        k0 = pl.multiple_of(c * KEY_CHUNK, KEY_CHUNK)
        lt = lt_ref[...]
        se = jnp.dot(ika_ref[pl.ds(k0, KEY_CHUNK), :], lt, preferred_element_type=F32)
        so = jnp.dot(ikb_ref[pl.ds(k0, KEY_CHUNK), :], lt, preferred_element_type=F32)
        te = jnp.maximum(se, 0.0) * w_even
        to = jnp.maximum(so, 0.0) * w_odd
        sc = jnp.zeros((KEY_CHUNK, Q_BLOCK), F32)
        for p in range(IDX_HEADS // 2):
            sc = sc + te[:, p * LANES:(p + 1) * LANES]
            sc = sc + to[:, p * LANES:(p + 1) * LANES]
        key = _sortable_key(sc)
        if masked:
            key = jnp.where(krow + k0 <= qpos, key, INT_MIN)
        hi_ref[pl.ds(k0, KEY_CHUNK), :] = (key >> 16).astype(I16)
        lo_ref[pl.ds(k0, KEY_CHUNK), :] = ((key & 0xFFFF) - 32768).astype(I16)

    def score_body(p, carry):
        score_chunk(2 * p, False)
        score_chunk(2 * p + 1, False)
        return carry

    lax.fori_loop(0, n512 - 1, score_body, 0)
    score_chunk(2 * (n512 - 1), True)
    score_chunk(2 * (n512 - 1) + 1, True)

    def count_ge(ref, cand_s):
        def step(j, cnt):
            r0 = pl.multiple_of(j * SEARCH_ROWS, SEARCH_ROWS)
            blk = ref[pl.ds(r0, SEARCH_ROWS), :].reshape(pk, 16, Q_BLOCK)
            ge = jnp.where(blk >= cand_s[None], one16, zero16)
            return cnt + _tree_sum([ge[k] for k in range(pk)])

        cnt = lax.fori_loop(0, n512, step, jnp.zeros((16, Q_BLOCK), I16))
        return jnp.sum(cnt.astype(I32), axis=0, keepdims=True)

    def search16(ref, need16):
        def search_pass(b, t_u):
            cand_u = t_u | jnp.left_shift(jnp.int32(1), 15 - b).astype(I16)
            tot = count_ge(ref, cand_u ^ min16)
            ok = jnp.broadcast_to(tot.astype(I16), (16, Q_BLOCK)) >= need16
            return jnp.where(ok, cand_u, t_u)

        return lax.fori_loop(0, 16, search_pass, jnp.zeros((16, Q_BLOCK), I16)) ^ min16

    thi = search16(hi_ref, jnp.full((16, Q_BLOCK), topk, I16))
    thi = jnp.where(thi < MIN16 + 1, jnp.int16(MIN16 + 1), thi)

    def band_step(j, cnt_gt):
        r0 = pl.multiple_of(j * SEARCH_ROWS, SEARCH_ROWS)
        h = hi_ref[pl.ds(r0, SEARCH_ROWS), :].reshape(pk, 16, Q_BLOCK)
        lw = lo_ref[pl.ds(r0, SEARCH_ROWS), :].reshape(pk, 16, Q_BLOCK)
        lo2 = jnp.where(h == thi[None], lw, min16)
        lo2_ref[pl.ds(r0, SEARCH_ROWS), :] = lo2.reshape(SEARCH_ROWS, Q_BLOCK)
        gt = jnp.where(h > thi[None], one16, zero16)
        return cnt_gt + _tree_sum([gt[k] for k in range(pk)])

    cnt_gt = lax.fori_loop(0, n512, band_step, jnp.zeros((16, Q_BLOCK), I16))
    need_lo = topk - jnp.sum(cnt_gt.astype(I32), axis=0, keepdims=True)
    tlo = search16(lo2_ref, jnp.broadcast_to(need_lo.astype(I16), (16, Q_BLOCK)))

    acc_ref[...] = jnp.zeros(acc_ref.shape, F32)
    m_ref[...] = jnp.full(m_ref.shape, NEG_BIG, F32)
    l_ref[...] = jnp.zeros(l_ref.shape, F32)
    ones_rows = jnp.ones((16, KEY_CHUNK), BF16)

    def qk_dots(c, st_ref):
        k0 = pl.multiple_of(c * KEY_CHUNK, KEY_CHUNK)
        for g in range(DSA_KV_HEADS):
            kch = dk_ref[pl.ds(k0, KEY_CHUNK), g * LANES:(g + 1) * LANES]
            qt = qt_ref[:, g * DSA_REP * LANES:(g + 1) * DSA_REP * LANES]
            st_ref[g] = jnp.dot(kch, qt, preferred_element_type=F32)

    def softmax_pv(c, st_ref):
        k0 = pl.multiple_of(c * KEY_CHUNK, KEY_CHUNK)
        h = hi_ref[pl.ds(k0, KEY_CHUNK), :].reshape(KEY_CHUNK // 16, 16, Q_BLOCK)
        lw = lo_ref[pl.ds(k0, KEY_CHUNK), :].reshape(KEY_CHUNK // 16, 16, Q_BLOCK)
        selv = jnp.where(h > thi[None], one16,
                         jnp.where(h == thi[None], jnp.where(lw >= tlo[None], one16, zero16), zero16))
        sel = selv.reshape(KEY_CHUNK, Q_BLOCK).astype(I32) != 0
        for g in range(DSA_KV_HEADS):
            st = st_ref[g]
            st = jnp.concatenate(
                [jnp.where(sel, st[:, r * LANES:(r + 1) * LANES], NEG_BIG) for r in range(DSA_REP)],
                axis=1)
            m_old = m_ref[g:g + 1, :]
            m_new = jnp.maximum(m_old, jnp.max(st, axis=0, keepdims=True))
            alpha = jnp.exp2(m_old - m_new)
            p = jnp.exp2(st - m_new)
            m_ref[g:g + 1, :] = m_new
            vt = jnp.concatenate([vt_ref[0, c, g * LANES:(g + 1) * LANES, :], ones_rows], axis=0)
            pv = jnp.dot(vt, p.astype(BF16), preferred_element_type=F32)
            acc_ref[g] = acc_ref[g] * alpha + pv[:HEAD_DIM]
            l_ref[g:g + 1, :] = alpha * l_ref[g:g + 1, :] + pv[HEAD_DIM:HEAD_DIM + 1]

    def attn_body(p, carry):
        c = 2 * p
        qk_dots(c + 1, stb_ref)
        softmax_pv(c, sta_ref)
        qk_dots(c + 2, sta_ref)
        softmax_pv(c + 1, stb_ref)
        return carry

    qk_dots(0, sta_ref)
    lax.fori_loop(0, n512 - 1, attn_body, 0)
    c_last = 2 * (n512 - 1)
    qk_dots(c_last + 1, stb_ref)
    softmax_pv(c_last, sta_ref)
    softmax_pv(c_last + 1, stb_ref)

    beta = beta_ref[...]
    for g in range(DSA_KV_HEADS):
        ot = acc_ref[g] / l_ref[g:g + 1, :]
        for r in range(DSA_REP):
            hh = g * DSA_REP + r
            blk = ot[:, r * LANES:(r + 1) * LANES].T
            o_ref[:, hh * LANES:(hh + 1) * LANES] = (
                blk * beta[:, hh * LANES:(hh + 1) * LANES]).astype(BF16)


def _dsa_call(P, vT, beta, *, B, L, topk):
    T = B * L
    nq = L // Q_BLOCK
    kern = functools.partial(_dsa_kernel, topk=topk)
    qrow = lambda col, width: pl.BlockSpec((Q_BLOCK, width),
                                           lambda b, i, col=col: (b * nq + i, col))
    tail128 = TAIL_COL // LANES
    return pl.pallas_call(
        kern,
        grid=(B, nq),
        in_specs=[
            qrow(4, P_TILE),
            qrow(5, P_TILE),
            qrow(tail128 + 6, LANES),
            pl.BlockSpec((L, DSA_KV_W), lambda b, i: (b, TAIL_COL // DSA_KV_W)),
            pl.BlockSpec((1, L // KEY_CHUNK, KEY_CHUNK, KEY_CHUNK), lambda b, i: (b, 0, 0, 0)),
            pl.BlockSpec((L, LANES), lambda b, i: (b, tail128 + 4)),
            pl.BlockSpec((L, LANES), lambda b, i: (b, tail128 + 5)),
            pl.BlockSpec((1, DSA_W), lambda b, i: (0, 0)),
        ],
        out_specs=pl.BlockSpec((Q_BLOCK, DSA_W), lambda b, i: (b * nq + i, 0)),
        out_shape=jax.ShapeDtypeStruct((T, DSA_W), BF16),
        scratch_shapes=[
            pltpu.VMEM((L, Q_BLOCK), I16),
            pltpu.VMEM((L, Q_BLOCK), I16),
            pltpu.VMEM((L, Q_BLOCK), I16),
            pltpu.VMEM((DSA_KV_HEADS, KEY_CHUNK, DSA_REP * Q_BLOCK), F32),
            pltpu.VMEM((DSA_KV_HEADS, KEY_CHUNK, DSA_REP * Q_BLOCK), F32),
            pltpu.VMEM((LANES, P_TILE), BF16),
            pltpu.VMEM((HEAD_DIM, DSA_W), BF16),
            pltpu.VMEM((DSA_KV_HEADS, HEAD_DIM, DSA_REP * Q_BLOCK), F32),
            pltpu.VMEM((DSA_KV_HEADS, DSA_REP * Q_BLOCK), F32),
            pltpu.VMEM((DSA_KV_HEADS, DSA_REP * Q_BLOCK), F32),
        ],
        compiler_params=_cparams(("arbitrary", "arbitrary")),
        name="dsa",
    )(P, P, P, P, vT, P, P, beta)


def _outproj_kernel(yr_ref, yd_ref, x_ref, mod_ref, w_ref, o_ref):
    mix = (jnp.dot(yr_ref[...], w_ref[0:RET_W, :], preferred_element_type=F32)
           + jnp.dot(yd_ref[...], w_ref[RET_W:RET_W + DSA_W, :], preferred_element_type=F32))
    o_ref[...] = x_ref[...] + mod_ref[0, 2:3, :] * mix


def _outproj_call(y_ret, y_dsa, x2, mod3, w_out, *, B, L, tm):
    T = B * L
    tpb = L // tm
    return pl.pallas_call(
        _outproj_kernel,
        grid=(T // tm,),
        in_specs=[
            pl.BlockSpec((tm, RET_W), lambda i: (i, 0)),
            pl.BlockSpec((tm, DSA_W), lambda i: (i, 0)),
            pl.BlockSpec((tm, D_MODEL), lambda i: (i, 0)),
            pl.BlockSpec((1, N_MOD, D_MODEL), lambda i: (i // tpb, 0, 0)),
            pl.BlockSpec((RET_W + DSA_W, D_MODEL), lambda i: (0, 0)),
        ],
        out_specs=pl.BlockSpec((tm, D_MODEL), lambda i: (i, 0)),
        out_shape=jax.ShapeDtypeStruct((T, D_MODEL), F32),
        compiler_params=_cparams(("arbitrary",)),
        name="outproj",
    )(y_ret, y_dsa, x2, mod3, w_out)


def _mlp_kernel(x_ref, mod_ref, g2_ref, w1_ref, w2_ref, o_ref, h_ref, *, nf):
    f = pl.program_id(1)

    @pl.when(f == 0)
    def _():
        x = x_ref[...]
        ms = jnp.mean(x * x, axis=-1, keepdims=True)
        y = (x * lax.rsqrt(ms + EPS)) * g2_ref[...]
        h = y * (1.0 + mod_ref[0, 4:5, :]) + mod_ref[0, 3:4, :]
        h_ref[...] = h.astype(BF16)

    hid = jnp.dot(h_ref[...], w1_ref[...], preferred_element_type=F32)
    a = jnp.maximum(hid, 0.0)
    a = (a * a).astype(BF16)
    for n in range(D_MODEL // MLP_NSLICE):
        cs = slice(n * MLP_NSLICE, (n + 1) * MLP_NSLICE)
        contrib = jnp.dot(a, w2_ref[:, cs], preferred_element_type=F32)

        @pl.when(f == 0)
        def _():
            o_ref[:, cs] = contrib

        @pl.when(jnp.logical_and(f > 0, f < nf - 1))
        def _():
            o_ref[:, cs] += contrib

        @pl.when(f == nf - 1)
        def _():
            o_ref[:, cs] = x_ref[:, cs] + mod_ref[0, 5:6, cs] * (o_ref[:, cs] + contrib)


def _mlp_call(x1, mod3, g2, w1, w2, *, B, L, tm, tf):
    T = B * L
    tpb = L // tm
    nf = D_FF // tf
    kern = functools.partial(_mlp_kernel, nf=nf)
    return pl.pallas_call(
        kern,
        grid=(T // tm, nf),
        in_specs=[
            pl.BlockSpec((tm, D_MODEL), lambda i, f: (i, 0)),
            pl.BlockSpec((1, N_MOD, D_MODEL), lambda i, f: (i // tpb, 0, 0)),
            pl.BlockSpec((1, D_MODEL), lambda i, f: (0, 0)),
            pl.BlockSpec((D_MODEL, tf), lambda i, f: (0, f)),
            pl.BlockSpec((tf, D_MODEL), lambda i, f: (f, 0)),
        ],
        out_specs=pl.BlockSpec((tm, D_MODEL), lambda i, f: (i, 0)),
        out_shape=jax.ShapeDtypeStruct((T, D_MODEL), F32),
        scratch_shapes=[pltpu.VMEM((tm, D_MODEL), BF16)],
        compiler_params=_cparams(("arbitrary", "arbitrary")),
        name="mlp",
    )(x1, mod3, g2, w1, w2)


def _deinterleave_heads(w):
    k = w.shape[0]
    return w.reshape(k, RET_HEADS, HEAD_DIM // 2, 2).transpose(0, 1, 3, 2).reshape(k, RET_W)


def _prep_w_in(w):
    o = np.cumsum([0, RET_W, RET_W, RET_W, RET_W, DSA_W, DSA_KV_W, DSA_KV_W,
                   IDX_HEADS * IDX_DIM, IDX_DIM, IDX_HEADS])
    part = lambda n: w[:, o[n]:o[n + 1]]
    z = lambda n: jnp.zeros((w.shape[0], n), w.dtype)
    cols = [_deinterleave_heads(part(0)), _deinterleave_heads(part(1)), part(2), part(3),
            part(4), part(7),
            part(5), part(6), part(8), z(LANES - IDX_DIM), z(LANES),
            part(9), z(LANES - IDX_HEADS), z(LANES)]
    return jnp.concatenate(cols, axis=1).astype(BF16)


def _rot_tables(L):
    pos = jnp.arange(L, dtype=F32)
    angle = 1.0 / (10000.0 ** jnp.linspace(0.0, 1.0, HEAD_DIM // 2, dtype=F32))
    theta = pos[:, None] * angle[None, :]
    s, c = jnp.sin(theta), jnp.cos(theta)
    return jnp.concatenate([c, c], axis=1), jnp.concatenate([-s, s], axis=1)


def _ret_tables():
    log_gamma = jnp.log(1.0 - jnp.exp2(-5.0 - jnp.arange(RET_HEADS, dtype=F32)))
    C = RET_CHUNK
    pos = jnp.arange(C, dtype=F32)
    diff = pos[:, None] - pos[None, :]
    dmask = jnp.where(diff[None] >= 0,
                      jnp.exp(log_gamma[:, None, None] * jnp.maximum(diff, 0.0)[None]), 0.0)
    xi = jnp.exp(log_gamma[:, None] * (pos[None, :] + 1.0))
    zeta = jnp.exp(log_gamma[:, None] * (C - 1.0 - pos[None, :]))
    bc = lambda a: jnp.broadcast_to(a[:, :, None], (RET_HEADS, C, C))
    return dmask, bc(xi), bc(zeta)


def _pad_lanes(v, n):
    return jnp.concatenate([v, jnp.zeros((n - v.shape[0],), v.dtype)]).reshape(1, n)


def _layer(x, c, w_ada, b_ada, norm1_g, norm2_g, w_in, ret_beta, q_norm_g, k_norm_g,
           idx_k_norm_g, dsa_beta, w_out, w_mlp1, w_mlp2):
    B, L, _ = x.shape
    topk = min(TOPK_MAX, L // 4)
    tm = min(1024, L)
    x2 = x.reshape(B * L, D_MODEL)

    mod3 = _mod_call(c, w_ada, b_ada).reshape(B, N_MOD, D_MODEL)

    cos_t, sin_t = _rot_tables(L)
    P, vT = _inproj_call(x2, mod3, norm1_g.reshape(1, -1), _prep_w_in(w_in), cos_t, sin_t,
                         q_norm_g.reshape(1, -1), k_norm_g.reshape(1, -1),
                         _pad_lanes(idx_k_norm_g, LANES), B=B, L=L, tm=tm)

    dmask, xi_b, zeta_b = _ret_tables()
    decay = tuple(float(v) for v in np.exp(_ret_consts() * np.float32(RET_CHUNK)))
    y_ret = _retention_call(P, dmask, xi_b, zeta_b, ret_beta.reshape(1, -1), decay,
                            B=B, L=L, rt=tm)
    y_dsa = _dsa_call(P, vT, dsa_beta.reshape(1, -1), B=B, L=L, topk=topk)

    x1 = _outproj_call(y_ret, y_dsa, x2, mod3, w_out.astype(BF16), B=B, L=L, tm=min(512, L))
    out = _mlp_call(x1, mod3, norm2_g.reshape(1, -1), w_mlp1.astype(BF16), w_mlp2.astype(BF16),
                    B=B, L=L, tm=tm, tf=512)
    return out.reshape(B, L, D_MODEL)


def kernel(x, c, w_ada, b_ada, norm1_g, norm2_g, w_in, ret_beta, q_norm_g, k_norm_g,
           idx_k_norm_g, dsa_beta, w_out, w_mlp1, w_mlp2):
    for l in range(w_ada.shape[0]):
        x = _layer(x, c, w_ada[l], b_ada[l], norm1_g[l], norm2_g[l], w_in[l], ret_beta[l],
                   q_norm_g[l], k_norm_g[l], idx_k_norm_g[l], dsa_beta[l], w_out[l],
                   w_mlp1[l], w_mlp2[l])
    return x
```

```python
import functools

import numpy as np
import jax
import jax.numpy as jnp
from jax import lax
from jax.experimental import pallas as pl
from jax.experimental.pallas import tpu as pltpu

F32 = jnp.float32
BF16 = jnp.bfloat16
I32 = jnp.int32
U32 = jnp.uint32

D_MODEL = 2048
HEAD_DIM = 128
RET_HEADS = 8
RET_W = 1024
RET_CHUNK = 128
DSA_HEADS = 8
DSA_KV_HEADS = 2
DSA_REP = DSA_HEADS // DSA_KV_HEADS
DSA_W = 1024
DSA_KV_W = 256
IDX_HEADS = 16
IDX_DIM = 64
TOPK_MAX = 256
Q_BLOCK = 128
D_FF = 4 * D_MODEL
N_MOD = 6
EPS = 1e-6

LANES = 128
KEY_CHUNK = 256
MLP_TF = 512
MLP_TN = 256
MLP_NA = D_FF // MLP_TF
MLP_NB = D_MODEL // MLP_TN
VMEM_LIMIT = 56 * 1024 * 1024

P_TILE = 1024
P_NTILES = 7
P_WIDTH = P_TILE * P_NTILES
TAIL_COL = 6 * P_TILE

POS_BITS = 14
INT_MIN = -(2 ** 31)
NEG_BIG = -1e30
LOG2E = 1.4426950408889634


def _cparams(sem):
    return pltpu.CompilerParams(dimension_semantics=sem, vmem_limit_bytes=VMEM_LIMIT)


def _mod_kernel(c_ref, w_ref, b_ref, o_ref):
    c = c_ref[...]
    s = c * jax.nn.sigmoid(c)
    o_ref[...] = jnp.dot(s.astype(BF16), w_ref[...].astype(BF16),
                         preferred_element_type=F32) + b_ref[...]


def _mod_call(c, w_ada, b_ada):
    B = c.shape[0]
    n = w_ada.shape[1]
    tn = 1536
    return pl.pallas_call(
        _mod_kernel,
        grid=(n // tn,),
        in_specs=[pl.BlockSpec((B, D_MODEL), lambda j: (0, 0)),
                  pl.BlockSpec((D_MODEL, tn), lambda j: (0, j)),
                  pl.BlockSpec((1, tn), lambda j: (0, j))],
        out_specs=pl.BlockSpec((B, tn), lambda j: (0, j)),
        out_shape=jax.ShapeDtypeStruct((B, n), F32),
        compiler_params=_cparams(("arbitrary",)),
        name="mod",
    )(c, w_ada, b_ada.reshape(1, n))


def _rms_scale(xs, width):
    ms = jnp.sum(xs * xs, axis=-1, keepdims=True) * (1.0 / width)
    return xs * lax.rsqrt(ms + EPS)


def _inproj_kernel(x_ref, mod_ref, g1_ref, w_ref, cos_ref, sin_ref, qg_ref, kg_ref, ikg_ref,
                   p_ref, vt_ref, h_ref, *, tm):
    j = pl.program_id(1)

    @pl.when(j == 0)
    def _():
        x = x_ref[...]
        ms = jnp.mean(x * x, axis=-1, keepdims=True)
        y = (x * lax.rsqrt(ms + EPS)) * g1_ref[...]
        h = y * (1.0 + mod_ref[0, 1:2, :]) + mod_ref[0, 0:1, :]
        h_ref[...] = h.astype(BF16)

    half = P_TILE // 2

    def acc_half(n):
        return jnp.dot(h_ref[...], w_ref[:, n * half:(n + 1) * half], preferred_element_type=F32)

    def rotate(scale):
        cos = cos_ref[...]
        sin = sin_ref[...]
        for n in range(2):
            acc = acc_half(n)
            for hh in range(half // LANES):
                xs = acc[:, hh * LANES:(hh + 1) * LANES]
                r = xs * cos + pltpu.roll(xs, LANES // 2, 1) * sin
                if scale != 1.0:
                    r = r * scale
                c0 = n * half + hh * LANES
                p_ref[:, c0:c0 + LANES] = r.astype(BF16)

    @pl.when(j == 0)
    def _():
        rotate(1.0)

    @pl.when(j == 1)
    def _():
        rotate(HEAD_DIM ** -0.5)

    @pl.when(j == 2)
    def _():
        for n in range(2):
            p_ref[:, n * half:(n + 1) * half] = acc_half(n).astype(BF16)

    @pl.when(j == 3)
    def _():
        for n in range(2):
            acc = acc_half(n)
            p_ref[:, n * half:(n + 1) * half] = (acc * jax.nn.sigmoid(acc)).astype(BF16)

    @pl.when(j == 4)
    def _():
        g = qg_ref[...] * (HEAD_DIM ** -0.5 * LOG2E)
        for n in range(2):
            acc = acc_half(n)
            for hh in range(half // LANES):
                xs = acc[:, hh * LANES:(hh + 1) * LANES]
                c0 = n * half + hh * LANES
                p_ref[:, c0:c0 + LANES] = (_rms_scale(xs, HEAD_DIM) * g).astype(BF16)

    @pl.when(j == 5)
    def _():
        for n in range(2):
            p_ref[:, n * half:(n + 1) * half] = (acc_half(n) * (IDX_DIM ** -0.5)).astype(BF16)

    @pl.when(j == 6)
    def _():
        kg = kg_ref[...]
        acc = acc_half(0)
        for g in range(DSA_KV_HEADS):
            xs = acc[:, g * LANES:(g + 1) * LANES]
            p_ref[:, g * LANES:(g + 1) * LANES] = (_rms_scale(xs, HEAD_DIM) * kg).astype(BF16)
        dv = acc[:, 256:512]
        p_ref[:, 256:512] = dv.astype(BF16)
        for s in range(tm // KEY_CHUNK):
            blk = dv[s * KEY_CHUNK:(s + 1) * KEY_CHUNK, :]
            vt_ref[0, s] = blk.T.astype(BF16)
        acc = acc_half(1)
        ik = acc[:, 0:128]
        ika = _rms_scale(ik, IDX_DIM) * ikg_ref[...]
        p_ref[:, 512:640] = ika.astype(BF16)
        p_ref[:, 640:768] = pltpu.roll(ika, LANES // 2, 1).astype(BF16)
        p_ref[:, 768:896] = (acc[:, 256:384] * (IDX_HEADS ** -0.5)).astype(BF16)
        p_ref[:, 896:1024] = jnp.zeros((tm, LANES), BF16)


def _inproj_call(x2, mod3, g1, w_p, cos_t, sin_t, qg, kg, ikg, *, B, L, tm):
    T = B * L
    tpb = L // tm
    kern = functools.partial(_inproj_kernel, tm=tm)
    return pl.pallas_call(
        kern,
        grid=(T // tm, P_NTILES),
        in_specs=[
            pl.BlockSpec((tm, D_MODEL), lambda i, j: (i, 0)),
            pl.BlockSpec((1, N_MOD, D_MODEL), lambda i, j: (i // tpb, 0, 0)),
            pl.BlockSpec((1, D_MODEL), lambda i, j: (0, 0)),
            pl.BlockSpec((D_MODEL, P_TILE), lambda i, j: (0, j)),
            pl.BlockSpec((tm, LANES), lambda i, j: (i % tpb, 0)),
            pl.BlockSpec((tm, LANES), lambda i, j: (i % tpb, 0)),
            pl.BlockSpec((1, LANES), lambda i, j: (0, 0)),
            pl.BlockSpec((1, LANES), lambda i, j: (0, 0)),
            pl.BlockSpec((1, LANES), lambda i, j: (0, 0)),
        ],
        out_specs=[
            pl.BlockSpec((tm, P_TILE), lambda i, j: (i, j)),
            pl.BlockSpec((1, tm // KEY_CHUNK, KEY_CHUNK, KEY_CHUNK),
                         lambda i, j: (i // tpb, i % tpb, 0, 0)),
        ],
        out_shape=[jax.ShapeDtypeStruct((T, P_WIDTH), BF16),
                   jax.ShapeDtypeStruct((B, L // KEY_CHUNK, KEY_CHUNK, KEY_CHUNK), BF16)],
        scratch_shapes=[pltpu.VMEM((tm, D_MODEL), BF16)],
        compiler_params=_cparams(("arbitrary", "arbitrary")),
        name="inproj",
    )(x2, mod3, g1, w_p, cos_t, sin_t, qg, kg, ikg)


def _ret_consts():
    h = np.arange(RET_HEADS, dtype=np.float32)
    log_gamma = np.log(np.float32(1.0) - np.exp2(np.float32(-5.0) - h)).astype(np.float32)
    return log_gamma


def _retention_kernel(q_ref, k_ref, v_ref, g_ref, dm_ref, xi_ref, zeta_ref, beta_ref,
                      o_ref, r_ref, *, rt, decay):
    @pl.when(pl.program_id(1) == 0)
    def _():
        r_ref[...] = jnp.zeros(r_ref.shape, F32)

    nt = (((1,), (1,)), ((), ()))

    def chunk(c, carry):
        r0 = pl.multiple_of(c * RET_CHUNK, RET_CHUNK)
        for hh in range(RET_HEADS):
            cs = slice(hh * LANES, (hh + 1) * LANES)
            q = q_ref[pl.ds(r0, RET_CHUNK), cs]
            k = k_ref[pl.ds(r0, RET_CHUNK), cs]
            v = v_ref[pl.ds(r0, RET_CHUNK), cs]
            inner = lax.dot_general(q, k, nt, preferred_element_type=F32) * dm_ref[hh]
            state = r_ref[hh]
            o = (jnp.dot(inner.astype(BF16), v, preferred_element_type=F32)
                 + jnp.dot(q, state.astype(BF16), preferred_element_type=F32) * xi_ref[hh])
            kz = k.astype(F32) * zeta_ref[hh]
            r_ref[hh] = state * decay[hh] + jnp.dot(kz.T.astype(BF16), v,
                                                    preferred_element_type=F32)
            mu = jnp.mean(o, axis=-1, keepdims=True)
            d = o - mu
            var = jnp.mean(d * d, axis=-1, keepdims=True)
            on = d * lax.rsqrt(var + EPS)
            y = g_ref[pl.ds(r0, RET_CHUNK), cs].astype(F32) * (on * beta_ref[:, cs])
            o_ref[pl.ds(r0, RET_CHUNK), cs] = y.astype(BF16)
        return carry

    lax.fori_loop(0, rt // RET_CHUNK, chunk, 0)


def _retention_call(P, dmask, xi_b, zeta_b, beta, decay, *, B, L, rt):
    T = B * L
    nb = L // rt
    kern = functools.partial(_retention_kernel, rt=rt, decay=decay)
    blk = lambda col: pl.BlockSpec((rt, P_TILE), lambda b, l, col=col: (b * nb + l, col))
    cst = pl.BlockSpec((RET_HEADS, RET_CHUNK, RET_CHUNK), lambda b, l: (0, 0, 0))
    return pl.pallas_call(
        kern,
        grid=(B, nb),
        in_specs=[blk(0), blk(1), blk(2), blk(3), cst, cst, cst,
                  pl.BlockSpec((1, RET_W), lambda b, l: (0, 0))],
        out_specs=pl.BlockSpec((rt, RET_W), lambda b, l: (b * nb + l, 0)),
        out_shape=jax.ShapeDtypeStruct((T, RET_W), BF16),
        scratch_shapes=[pltpu.VMEM((RET_HEADS, HEAD_DIM, HEAD_DIM), F32)],
        compiler_params=_cparams(("arbitrary", "arbitrary")),
        name="retention",
    )(P, P, P, P, dmask, xi_b, zeta_b, beta)


def _sortable_key(x):
    b = pltpu.bitcast(x, I32)
    return b ^ ((b >> 31) & 0x7FFFFFFF)


def _tree_sum(parts):
    parts = list(parts)
    while len(parts) > 1:
        nxt = [parts[k] + parts[k + 1] for k in range(0, len(parts) - 1, 2)]
        if len(parts) % 2:
            nxt.append(parts[-1])
        parts = nxt
    return parts[0]


def _bit_transpose32(a):
    a = list(a)
    for k, m in ((16, 0x0000FFFF), (8, 0x00FF00FF), (4, 0x0F0F0F0F), (2, 0x33333333),
                 (1, 0x55555555)):
        for i in range(32):
            if i & k == 0:
                t = ((a[i] >> k) ^ a[i + k]) & jnp.uint32(m)
                a[i + k] = a[i + k] ^ t
                a[i] = a[i] ^ (t << k)
    return a


def _dsa_kernel(dq_ref, iq_ref, iw_ref, dk_ref, vt_ref, ika_ref, ikb_ref, beta_ref, o_ref,
                key_ref, plane_ref, eq_ref, sta_ref, stb_ref, lt_ref, qt_ref, acc_ref, m_ref, l_ref,
                *, topk, nblk):
    i = pl.program_id(1)
    n256 = (i + 2) // 2
    npair = (n256 + 1) // 2
    q0 = i * Q_BLOCK

    iq = iq_ref[...].astype(F32)
    for p in range(IDX_HEADS // 2):
        lt_ref[:, p * LANES:(p + 1) * LANES] = iq[:, p * LANES:(p + 1) * LANES].T.astype(BF16)
    dq = dq_ref[...].astype(F32)
    for hh in range(DSA_HEADS):
        qt_ref[:, hh * LANES:(hh + 1) * LANES] = dq[:, hh * LANES:(hh + 1) * LANES].T.astype(BF16)
    iwt = iw_ref[...].astype(F32).T
    w_even = jnp.concatenate([iwt[2 * p:2 * p + 1, :] for p in range(IDX_HEADS // 2)], axis=1)
    w_odd = jnp.concatenate([iwt[2 * p + 1:2 * p + 2, :] for p in range(IDX_HEADS // 2)], axis=1)

    qpos = q0 + lax.broadcasted_iota(I32, (KEY_CHUNK, Q_BLOCK), 1)
    krow = lax.broadcasted_iota(I32, (KEY_CHUNK, Q_BLOCK), 0)

    def score_chunk(c, masked):
        k0 = pl.multiple_of(c * KEY_CHUNK, KEY_CHUNK)
        lt = lt_ref[...]
        se = jnp.dot(ika_ref[pl.ds(k0, KEY_CHUNK), :], lt, preferred_element_type=F32)
        so = jnp.dot(ikb_ref[pl.ds(k0, KEY_CHUNK), :], lt, preferred_element_type=F32)
        te = jnp.maximum(se, 0.0) * w_even
        to = jnp.maximum(so, 0.0) * w_odd
        sc = jnp.zeros((KEY_CHUNK, Q_BLOCK), F32)
        for p in range(IDX_HEADS // 2):
            sc = sc + te[:, p * LANES:(p + 1) * LANES]
            sc = sc + to[:, p * LANES:(p + 1) * LANES]
        key = _sortable_key(sc)
        if masked:
            key = jnp.where(krow + k0 <= qpos, key, INT_MIN)
        key_ref[pl.ds(k0, KEY_CHUNK), :] = key
        u = pltpu.bitcast(key ^ INT_MIN, U32)
        planes = _bit_transpose32([u[8 * j:8 * j + 8, :] for j in range(32)])
        for b in range(32):
            plane_ref[b, c] = planes[b]

    def score_body(p, carry):
        score_chunk(2 * p, False)
        score_chunk(2 * p + 1, False)
        return carry

    lax.fori_loop(0, npair - 1, score_body, 0)
    score_chunk(2 * (npair - 1), True)
    score_chunk(2 * (npair - 1) + 1, True)

    blk_id = lax.broadcasted_iota(I32, (nblk, 8, Q_BLOCK), 0)
    eq_ref[...] = jnp.where(blk_id < 2 * npair, jnp.uint32(0xFFFFFFFF), jnp.uint32(0))

    def search_pass(k, carry):
        t_u, cnt_gt = carry
        b = 31 - k
        eq = eq_ref[...]
        t = eq & plane_ref[b]
        pc = lax.population_count(t)
        tot = jnp.sum(_tree_sum([pc[j] for j in range(nblk)]).astype(I32), axis=0, keepdims=True)
        tot = jnp.broadcast_to(tot, (8, Q_BLOCK))
        take = cnt_gt + tot >= topk
        keep_eq = jnp.where(take, jnp.uint32(0), jnp.uint32(0xFFFFFFFF))
        eq_ref[...] = (eq & keep_eq[None]) ^ t
        bit = jnp.left_shift(jnp.int32(1), b)
        return (jnp.where(take, t_u | bit, t_u), jnp.where(take, cnt_gt, cnt_gt + tot))

    t_u, cnt_gt = lax.fori_loop(0, 32, search_pass,
                                (jnp.zeros((8, Q_BLOCK), I32), jnp.zeros((8, Q_BLOCK), I32)))
    thr = jnp.maximum(t_u[0:1, :] ^ INT_MIN, INT_MIN + 1)

    pc_eq = lax.population_count(eq_ref[...])
    n_eq = jnp.sum(_tree_sum([pc_eq[j] for j in range(nblk)]).astype(I32), axis=0, keepdims=True)
    excess = cnt_gt[0:1, :] + n_eq - topk
    has_ties = jnp.max(excess) > 0

    @pl.when(has_ties)
    def _():
        keep = jnp.where(excess > 0, topk - cnt_gt[0:1, :], jnp.int32(2 ** 30))
        rows = lax.broadcasted_iota(I32, (KEY_CHUNK, Q_BLOCK), 0)

        def count_tied_below(q_cut):
            def step(c, cnt):
                k0 = pl.multiple_of(c * KEY_CHUNK, KEY_CHUNK)
                kk = key_ref[pl.ds(k0, KEY_CHUNK), :]
                hit = jnp.where(kk == thr, jnp.where(rows + k0 < q_cut, 1, 0), 0)
                return cnt + jnp.sum(hit, axis=0, keepdims=True)
            return lax.fori_loop(0, 2 * npair, step, jnp.zeros((1, Q_BLOCK), I32))

        def pos_pass(k, q_max):
            cand = q_max | jnp.left_shift(jnp.int32(1), POS_BITS - 1 - k)
            return jnp.where(count_tied_below(cand) < keep, cand, q_max)

        q_max = lax.fori_loop(0, POS_BITS, pos_pass, jnp.zeros((1, Q_BLOCK), I32))

        def drop_step(c, carry):
            k0 = pl.multiple_of(c * KEY_CHUNK, KEY_CHUNK)
            kk = key_ref[pl.ds(k0, KEY_CHUNK), :]
            drop = jnp.where(kk == thr, jnp.where(rows + k0 > q_max, 1, 0), 0)
            key_ref[pl.ds(k0, KEY_CHUNK), :] = jnp.where(drop > 0, INT_MIN, kk)
            return carry

        lax.fori_loop(0, 2 * npair, drop_step, 0)

    acc_ref[...] = jnp.zeros(acc_ref.shape, F32)
    m_ref[...] = jnp.full(m_ref.shape, NEG_BIG, F32)
    l_ref[...] = jnp.zeros(l_ref.shape, F32)
    ones_rows = jnp.ones((16, KEY_CHUNK), BF16)

    def qk_dots(c, st_ref):
        k0 = pl.multiple_of(c * KEY_CHUNK, KEY_CHUNK)
        for g in range(DSA_KV_HEADS):
            kch = dk_ref[pl.ds(k0, KEY_CHUNK), g * LANES:(g + 1) * LANES]
            qt = qt_ref[:, g * DSA_REP * LANES:(g + 1) * DSA_REP * LANES]
            st_ref[g] = jnp.dot(kch, qt, preferred_element_type=F32)

    def softmax_pv(c, st_ref):
        k0 = pl.multiple_of(c * KEY_CHUNK, KEY_CHUNK)
        sel = key_ref[pl.ds(k0, KEY_CHUNK), :] >= thr
        for g in range(DSA_KV_HEADS):
            st = st_ref[g]
            st = jnp.concatenate(
                [jnp.where(sel, st[:, r * LANES:(r + 1) * LANES], NEG_BIG) for r in range(DSA_REP)],
                axis=1)
            m_old = m_ref[g:g + 1, :]
            m_new = jnp.maximum(m_old, jnp.max(st, axis=0, keepdims=True))
            alpha = jnp.exp2(m_old - m_new)
            p = jnp.exp2(st - m_new)
            m_ref[g:g + 1, :] = m_new
            vt = jnp.concatenate([vt_ref[0, c, g * LANES:(g + 1) * LANES, :], ones_rows], axis=0)
            pv = jnp.dot(vt, p.astype(BF16), preferred_element_type=F32)
            acc_ref[g] = acc_ref[g] * alpha + pv[:HEAD_DIM]
            l_ref[g:g + 1, :] = alpha * l_ref[g:g + 1, :] + pv[HEAD_DIM:HEAD_DIM + 1]

    def attn_body(p, carry):
        c = 2 * p
        qk_dots(c + 1, stb_ref)
        softmax_pv(c, sta_ref)
        qk_dots(c + 2, sta_ref)
        softmax_pv(c + 1, stb_ref)
        return carry

    qk_dots(0, sta_ref)
    lax.fori_loop(0, npair - 1, attn_body, 0)
    c_last = 2 * (npair - 1)
    qk_dots(c_last + 1, stb_ref)
    softmax_pv(c_last, sta_ref)
    softmax_pv(c_last + 1, stb_ref)

    beta = beta_ref[...]
    for g in range(DSA_KV_HEADS):
        ot = acc_ref[g] / l_ref[g:g + 1, :]
        for r in range(DSA_REP):
            hh = g * DSA_REP + r
            blk = ot[:, r * LANES:(r + 1) * LANES].T
            o_ref[:, hh * LANES:(hh + 1) * LANES] = (
                blk * beta[:, hh * LANES:(hh + 1) * LANES]).astype(BF16)


def _dsa_call(P, vT, beta, *, B, L, topk):
    T = B * L
    nq = L // Q_BLOCK
    nblk = L // KEY_CHUNK
    kern = functools.partial(_dsa_kernel, topk=topk, nblk=nblk)
    qrow = lambda col, width: pl.BlockSpec((Q_BLOCK, width),
                                           lambda b, i, col=col: (b * nq + i, col))
    tail128 = TAIL_COL // LANES
    return pl.pallas_call(
        kern,
        grid=(B, nq),
        in_specs=[
            qrow(4, P_TILE),
            qrow(5, P_TILE),
            qrow(tail128 + 6, LANES),
            pl.BlockSpec((L, DSA_KV_W), lambda b, i: (b, TAIL_COL // DSA_KV_W)),
            pl.BlockSpec((1, L // KEY_CHUNK, KEY_CHUNK, KEY_CHUNK), lambda b, i: (b, 0, 0, 0)),
            pl.BlockSpec((L, LANES), lambda b, i: (b, tail128 + 4)),
            pl.BlockSpec((L, LANES), lambda b, i: (b, tail128 + 5)),
            pl.BlockSpec((1, DSA_W), lambda b, i: (0, 0)),
        ],
        out_specs=pl.BlockSpec((Q_BLOCK, DSA_W), lambda b, i: (b * nq + i, 0)),
        out_shape=jax.ShapeDtypeStruct((T, DSA_W), BF16),
        scratch_shapes=[
            pltpu.VMEM((L, Q_BLOCK), I32),
            pltpu.VMEM((32, nblk, 8, Q_BLOCK), U32),
            pltpu.VMEM((nblk, 8, Q_BLOCK), U32),
            pltpu.VMEM((DSA_KV_HEADS, KEY_CHUNK, DSA_REP * Q_BLOCK), F32),
            pltpu.VMEM((DSA_KV_HEADS, KEY_CHUNK, DSA_REP * Q_BLOCK), F32),
            pltpu.VMEM((LANES, P_TILE), BF16),
            pltpu.VMEM((HEAD_DIM, DSA_W), BF16),
            pltpu.VMEM((DSA_KV_HEADS, HEAD_DIM, DSA_REP * Q_BLOCK), F32),
            pltpu.VMEM((DSA_KV_HEADS, DSA_REP * Q_BLOCK), F32),
            pltpu.VMEM((DSA_KV_HEADS, DSA_REP * Q_BLOCK), F32),
        ],
        compiler_params=_cparams(("arbitrary", "arbitrary")),
        name="dsa",
    )(P, P, P, P, vT, P, P, beta)


def _outproj_kernel(yr_ref, yd_ref, x_ref, mod_ref, w_ref, o_ref):
    mix = (jnp.dot(yr_ref[...], w_ref[0:RET_W, :], preferred_element_type=F32)
           + jnp.dot(yd_ref[...], w_ref[RET_W:RET_W + DSA_W, :], preferred_element_type=F32))
    o_ref[...] = x_ref[...] + mod_ref[0, 2:3, :] * mix


def _outproj_call(y_ret, y_dsa, x2, mod3, w_out, *, B, L, tm):
    T = B * L
    tpb = L // tm
    return pl.pallas_call(
        _outproj_kernel,
        grid=(T // tm,),
        in_specs=[
            pl.BlockSpec((tm, RET_W), lambda i: (i, 0)),
            pl.BlockSpec((tm, DSA_W), lambda i: (i, 0)),
            pl.BlockSpec((tm, D_MODEL), lambda i: (i, 0)),
            pl.BlockSpec((1, N_MOD, D_MODEL), lambda i: (i // tpb, 0, 0)),
            pl.BlockSpec((RET_W + DSA_W, D_MODEL), lambda i: (0, 0)),
        ],
        out_specs=pl.BlockSpec((tm, D_MODEL), lambda i: (i, 0)),
        out_shape=jax.ShapeDtypeStruct((T, D_MODEL), F32),
        compiler_params=_cparams(("arbitrary",)),
        name="outproj",
    )(y_ret, y_dsa, x2, mod3, w_out)


def _mlp_kernel(x_ref, xs_ref, mod_ref, gate_ref, g2_ref, w1_ref, w2_ref, o_ref, h_ref, a_ref,
                *, tm):
    s = pl.program_id(1)

    @pl.when(s == 0)
    def _():
        x = x_ref[...]
        ms = jnp.mean(x * x, axis=-1, keepdims=True)
        y = (x * lax.rsqrt(ms + EPS)) * g2_ref[...]
        h = y * (1.0 + mod_ref[0, 4:5, :]) + mod_ref[0, 3:4, :]
        h_ref[...] = h.astype(BF16)

    @pl.when(s < MLP_NA)
    def _():
        hid = jnp.dot(h_ref[...], w1_ref[...], preferred_element_type=F32)
        a = jnp.maximum(hid, 0.0)
        a_ref[s] = (a * a).astype(BF16)

    @pl.when(s >= MLP_NA)
    def _():
        half = tm // 2
        for r in range(2):
            rows = slice(r * half, (r + 1) * half)
            y = None
            for f in range(MLP_NA):
                d = jnp.dot(a_ref[f, rows, :], w2_ref[f * MLP_TF:(f + 1) * MLP_TF, :],
                            preferred_element_type=F32)
                y = d if y is None else y + d
            o_ref[rows, :] = xs_ref[rows, :] + gate_ref[0] * y


def _mlp_call(x1, mod3, g2, w1, w2, *, B, L, tm):
    T = B * L
    tpb = L // tm
    gate2 = mod3[:, 5:6, :]
    kern = functools.partial(_mlp_kernel, tm=tm)
    ncol = lambda s: jnp.maximum(s - MLP_NA, 0)
    return pl.pallas_call(
        kern,
        grid=(T // tm, MLP_NA + MLP_NB),
        in_specs=[
            pl.BlockSpec((tm, D_MODEL), lambda i, s: (i, 0), pipeline_mode=pl.Buffered(1)),
            pl.BlockSpec((tm, MLP_TN), lambda i, s: (i, ncol(s))),
            pl.BlockSpec((1, N_MOD, D_MODEL), lambda i, s: (i // tpb, 0, 0)),
            pl.BlockSpec((1, 1, MLP_TN), lambda i, s: (i // tpb, 0, ncol(s))),
            pl.BlockSpec((1, D_MODEL), lambda i, s: (0, 0)),
            pl.BlockSpec((D_MODEL, MLP_TF), lambda i, s: (0, jnp.minimum(s, MLP_NA - 1))),
            pl.BlockSpec((D_FF, MLP_TN), lambda i, s: (0, ncol(s))),
        ],
        out_specs=pl.BlockSpec((tm, MLP_TN), lambda i, s: (i, ncol(s))),
        out_shape=jax.ShapeDtypeStruct((T, D_MODEL), F32),
        scratch_shapes=[pltpu.VMEM((tm, D_MODEL), BF16),
                        pltpu.VMEM((MLP_NA, tm, MLP_TF), BF16)],
        compiler_params=_cparams(("arbitrary", "arbitrary")),
        name="mlp",
    )(x1, x1, mod3, gate2, g2, w1, w2)


def _deinterleave_heads(w):
    k = w.shape[0]
    return w.reshape(k, RET_HEADS, HEAD_DIM // 2, 2).transpose(0, 1, 3, 2).reshape(k, RET_W)


def _prep_w_in(w):
    o = np.cumsum([0, RET_W, RET_W, RET_W, RET_W, DSA_W, DSA_KV_W, DSA_KV_W,
                   IDX_HEADS * IDX_DIM, IDX_DIM, IDX_HEADS])
    part = lambda n: w[:, o[n]:o[n + 1]]
    z = lambda n: jnp.zeros((w.shape[0], n), w.dtype)
    cols = [_deinterleave_heads(part(0)), _deinterleave_heads(part(1)), part(2), part(3),
            part(4), part(7),
            part(5), part(6), part(8), z(LANES - IDX_DIM), z(LANES),
            part(9), z(LANES - IDX_HEADS), z(LANES)]
    return jnp.concatenate(cols, axis=1).astype(BF16)


def _rot_tables(L):
    pos = jnp.arange(L, dtype=F32)
    angle = 1.0 / (10000.0 ** jnp.linspace(0.0, 1.0, HEAD_DIM // 2, dtype=F32))
    theta = pos[:, None] * angle[None, :]
    s, c = jnp.sin(theta), jnp.cos(theta)
    return jnp.concatenate([c, c], axis=1), jnp.concatenate([-s, s], axis=1)


def _ret_tables():
    log_gamma = jnp.log(1.0 - jnp.exp2(-5.0 - jnp.arange(RET_HEADS, dtype=F32)))
    C = RET_CHUNK
    pos = jnp.arange(C, dtype=F32)
    diff = pos[:, None] - pos[None, :]
    dmask = jnp.where(diff[None] >= 0,
                      jnp.exp(log_gamma[:, None, None] * jnp.maximum(diff, 0.0)[None]), 0.0)
    xi = jnp.exp(log_gamma[:, None] * (pos[None, :] + 1.0))
    zeta = jnp.exp(log_gamma[:, None] * (C - 1.0 - pos[None, :]))
    bc = lambda a: jnp.broadcast_to(a[:, :, None], (RET_HEADS, C, C))
    return dmask, bc(xi), bc(zeta)


def _pad_lanes(v, n):
    return jnp.concatenate([v, jnp.zeros((n - v.shape[0],), v.dtype)]).reshape(1, n)


def _layer(x, c, w_ada, b_ada, norm1_g, norm2_g, w_in, ret_beta, q_norm_g, k_norm_g,
           idx_k_norm_g, dsa_beta, w_out, w_mlp1, w_mlp2):
    B, L, _ = x.shape
    assert L % (2 * KEY_CHUNK) == 0 and L <= 2 ** (POS_BITS - 1)
    topk = min(TOPK_MAX, L // 4)
    tm = min(1024, L)
    x2 = x.reshape(B * L, D_MODEL)

    mod3 = _mod_call(c, w_ada, b_ada).reshape(B, N_MOD, D_MODEL)

    cos_t, sin_t = _rot_tables(L)
    P, vT = _inproj_call(x2, mod3, norm1_g.reshape(1, -1), _prep_w_in(w_in), cos_t, sin_t,
                         q_norm_g.reshape(1, -1), k_norm_g.reshape(1, -1),
                         _pad_lanes(idx_k_norm_g, LANES), B=B, L=L, tm=tm)

    dmask, xi_b, zeta_b = _ret_tables()
    decay = tuple(float(v) for v in np.exp(_ret_consts() * np.float32(RET_CHUNK)))
    y_ret = _retention_call(P, dmask, xi_b, zeta_b, ret_beta.reshape(1, -1), decay,
                            B=B, L=L, rt=tm)
    y_dsa = _dsa_call(P, vT, dsa_beta.reshape(1, -1), B=B, L=L, topk=topk)

    x1 = _outproj_call(y_ret, y_dsa, x2, mod3, w_out.astype(BF16), B=B, L=L, tm=min(512, L))
    out = _mlp_call(x1, mod3, norm2_g.reshape(1, -1), w_mlp1.astype(BF16), w_mlp2.astype(BF16),
                    B=B, L=L, tm=tm)
    return out.reshape(B, L, D_MODEL)


def kernel(x, c, w_ada, b_ada, norm1_g, norm2_g, w_in, ret_beta, q_norm_g, k_norm_g,
           idx_k_norm_g, dsa_beta, w_out, w_mlp1, w_mlp2):
    for l in range(w_ada.shape[0]):
        x = _layer(x, c, w_ada[l], b_ada[l], norm1_g[l], norm2_g[l], w_in[l], ret_beta[l],
                   q_norm_g[l], k_norm_g[l], idx_k_norm_g[l], dsa_beta[l], w_out[l],
                   w_mlp1[l], w_mlp2[l])
    return x
```

```python
import functools

import numpy as np
import jax
import jax.numpy as jnp
from jax import lax
from jax.experimental import pallas as pl
from jax.experimental.pallas import tpu as pltpu

F32 = jnp.float32
BF16 = jnp.bfloat16
I32 = jnp.int32
U32 = jnp.uint32

D_MODEL = 2048
HEAD_DIM = 128
RET_HEADS = 8
RET_W = 1024
RET_CHUNK = 128
RET_BLOCK = 256
DSA_HEADS = 8
DSA_KV_HEADS = 2
DSA_REP = DSA_HEADS // DSA_KV_HEADS
DSA_W = 1024
DSA_KV_W = 256
IDX_HEADS = 16
IDX_DIM = 64
TOPK_MAX = 256
Q_BLOCK = 128
D_FF = 4 * D_MODEL
N_MOD = 6
EPS = 1e-6

LANES = 128
KEY_CHUNK = 256
MLP_TF = 512
MLP_TN = 256
MLP_NA = D_FF // MLP_TF
MLP_NB = D_MODEL // MLP_TN
VMEM_LIMIT = 56 * 1024 * 1024

P_TILE = 1024
P_NTILES = 7
P_WIDTH = P_TILE * P_NTILES
TAIL_COL = 6 * P_TILE

POS_BITS = 14
INT_MIN = -(2 ** 31)
NEG_BIG = -1e30
LOG2E = 1.4426950408889634


def _cparams(sem):
    return pltpu.CompilerParams(dimension_semantics=sem, vmem_limit_bytes=VMEM_LIMIT)


def _mod_kernel(c_ref, w_ref, b_ref, o_ref):
    c = c_ref[...]
    s = c * jax.nn.sigmoid(c)
    o_ref[...] = jnp.dot(s.astype(BF16), w_ref[...].astype(BF16),
                         preferred_element_type=F32) + b_ref[...]


def _mod_call(c, w_ada, b_ada):
    B = c.shape[0]
    n = w_ada.shape[1]
    tn = 1536
    return pl.pallas_call(
        _mod_kernel,
        grid=(n // tn,),
        in_specs=[pl.BlockSpec((B, D_MODEL), lambda j: (0, 0)),
                  pl.BlockSpec((D_MODEL, tn), lambda j: (0, j)),
                  pl.BlockSpec((1, tn), lambda j: (0, j))],
        out_specs=pl.BlockSpec((B, tn), lambda j: (0, j)),
        out_shape=jax.ShapeDtypeStruct((B, n), F32),
        compiler_params=_cparams(("arbitrary",)),
        name="mod",
    )(c, w_ada, b_ada.reshape(1, n))


def _rms_scale(xs, width):
    ms = jnp.sum(xs * xs, axis=-1, keepdims=True) * (1.0 / width)
    return xs * lax.rsqrt(ms + EPS)


def _inproj_kernel(x_ref, mod_ref, g1_ref, w_ref, cos_ref, sin_ref, qg_ref, kg_ref, ikg_ref,
                   p_ref, vt_ref, h_ref, *, tm):
    j = pl.program_id(1)

    @pl.when(j == 0)
    def _():
        x = x_ref[...]
        ms = jnp.mean(x * x, axis=-1, keepdims=True)
        y = (x * lax.rsqrt(ms + EPS)) * g1_ref[...]
        h = y * (1.0 + mod_ref[0, 1:2, :]) + mod_ref[0, 0:1, :]
        h_ref[...] = h.astype(BF16)

    half = P_TILE // 2

    def acc_half(n):
        return jnp.dot(h_ref[...], w_ref[:, n * half:(n + 1) * half], preferred_element_type=F32)

    def rotate(scale):
        cos = cos_ref[...]
        sin = sin_ref[...]
        for n in range(2):
            acc = acc_half(n)
            for hh in range(half // LANES):
                xs = acc[:, hh * LANES:(hh + 1) * LANES]
                r = xs * cos + pltpu.roll(xs, LANES // 2, 1) * sin
                if scale != 1.0:
                    r = r * scale
                c0 = n * half + hh * LANES
                p_ref[:, c0:c0 + LANES] = r.astype(BF16)

    @pl.when(j == 0)
    def _():
        rotate(1.0)

    @pl.when(j == 1)
    def _():
        rotate(HEAD_DIM ** -0.5)

    @pl.when(j == 2)
    def _():
        for n in range(2):
            p_ref[:, n * half:(n + 1) * half] = acc_half(n).astype(BF16)

    @pl.when(j == 3)
    def _():
        for n in range(2):
            acc = acc_half(n)
            p_ref[:, n * half:(n + 1) * half] = (acc * jax.nn.sigmoid(acc)).astype(BF16)

    @pl.when(j == 4)
    def _():
        g = qg_ref[...] * (HEAD_DIM ** -0.5 * LOG2E)
        for n in range(2):
            acc = acc_half(n)
            for hh in range(half // LANES):
                xs = acc[:, hh * LANES:(hh + 1) * LANES]
                c0 = n * half + hh * LANES
                p_ref[:, c0:c0 + LANES] = (_rms_scale(xs, HEAD_DIM) * g).astype(BF16)

    @pl.when(j == 5)
    def _():
        for n in range(2):
            p_ref[:, n * half:(n + 1) * half] = (acc_half(n) * (IDX_DIM ** -0.5)).astype(BF16)

    @pl.when(j == 6)
    def _():
        kg = kg_ref[...]
        acc = acc_half(0)
        for g in range(DSA_KV_HEADS):
            xs = acc[:, g * LANES:(g + 1) * LANES]
            p_ref[:, g * LANES:(g + 1) * LANES] = (_rms_scale(xs, HEAD_DIM) * kg).astype(BF16)
        dv = acc[:, 256:512]
        p_ref[:, 256:512] = dv.astype(BF16)
        for s in range(tm // KEY_CHUNK):
            blk = dv[s * KEY_CHUNK:(s + 1) * KEY_CHUNK, :]
            vt_ref[0, s] = blk.T.astype(BF16)
        acc = acc_half(1)
        ik = acc[:, 0:128]
        ika = _rms_scale(ik, IDX_DIM) * ikg_ref[...]
        p_ref[:, 512:640] = ika.astype(BF16)
        p_ref[:, 640:768] = pltpu.roll(ika, LANES // 2, 1).astype(BF16)
        p_ref[:, 768:896] = (acc[:, 256:384] * (IDX_HEADS ** -0.5)).astype(BF16)
        p_ref[:, 896:1024] = jnp.zeros((tm, LANES), BF16)


def _inproj_call(x2, mod3, g1, w_p, cos_t, sin_t, qg, kg, ikg, *, B, L, tm):
    T = B * L
    tpb = L // tm
    kern = functools.partial(_inproj_kernel, tm=tm)
    return pl.pallas_call(
        kern,
        grid=(T // tm, P_NTILES),
        in_specs=[
            pl.BlockSpec((tm, D_MODEL), lambda i, j: (i, 0)),
            pl.BlockSpec((1, N_MOD, D_MODEL), lambda i, j: (i // tpb, 0, 0)),
            pl.BlockSpec((1, D_MODEL), lambda i, j: (0, 0)),
            pl.BlockSpec((D_MODEL, P_TILE), lambda i, j: (0, j)),
            pl.BlockSpec((tm, LANES), lambda i, j: (i % tpb, 0)),
            pl.BlockSpec((tm, LANES), lambda i, j: (i % tpb, 0)),
            pl.BlockSpec((1, LANES), lambda i, j: (0, 0)),
            pl.BlockSpec((1, LANES), lambda i, j: (0, 0)),
            pl.BlockSpec((1, LANES), lambda i, j: (0, 0)),
        ],
        out_specs=[
            pl.BlockSpec((tm, P_TILE), lambda i, j: (i, j)),
            pl.BlockSpec((1, tm // KEY_CHUNK, KEY_CHUNK, KEY_CHUNK),
                         lambda i, j: (i // tpb, i % tpb, 0, 0)),
        ],
        out_shape=[jax.ShapeDtypeStruct((T, P_WIDTH), BF16),
                   jax.ShapeDtypeStruct((B, L // KEY_CHUNK, KEY_CHUNK, KEY_CHUNK), BF16)],
        scratch_shapes=[pltpu.VMEM((tm, D_MODEL), BF16)],
        compiler_params=_cparams(("arbitrary", "arbitrary")),
        name="inproj",
    )(x2, mod3, g1, w_p, cos_t, sin_t, qg, kg, ikg)


def _ret_consts():
    h = np.arange(RET_HEADS, dtype=np.float32)
    log_gamma = np.log(np.float32(1.0) - np.exp2(np.float32(-5.0) - h)).astype(np.float32)
    return log_gamma


def _retention_kernel(q_ref, k_ref, v_ref, g_ref, dm_ref, xi_ref, zeta_ref, beta_ref,
                      o_ref, r_ref, *, rt, decay):
    @pl.when(pl.program_id(1) == 0)
    def _():
        r_ref[...] = jnp.zeros(r_ref.shape, F32)

    nt = (((1,), (1,)), ((), ()))

    def chunk(c, carry):
        r0 = pl.multiple_of(c * RET_BLOCK, RET_BLOCK)
        for hh in range(RET_HEADS):
            cs = slice(hh * LANES, (hh + 1) * LANES)
            q = q_ref[pl.ds(r0, RET_BLOCK), cs]
            k = k_ref[pl.ds(r0, RET_BLOCK), cs]
            v = v_ref[pl.ds(r0, RET_BLOCK), cs]
            inner = lax.dot_general(q, k, nt, preferred_element_type=F32) * dm_ref[hh]
            state = r_ref[hh]
            o = (jnp.dot(inner.astype(BF16), v, preferred_element_type=F32)
                 + jnp.dot(q, state.astype(BF16), preferred_element_type=F32) * xi_ref[hh])
            kz = k.astype(F32) * zeta_ref[hh]
            r_ref[hh] = state * decay[hh] + jnp.dot(kz.T.astype(BF16), v,
                                                    preferred_element_type=F32)
            mu = jnp.mean(o, axis=-1, keepdims=True)
            d = o - mu
            var = jnp.mean(d * d, axis=-1, keepdims=True)
            on = d * lax.rsqrt(var + EPS)
            y = g_ref[pl.ds(r0, RET_BLOCK), cs].astype(F32) * (on * beta_ref[:, cs])
            o_ref[pl.ds(r0, RET_BLOCK), cs] = y.astype(BF16)
        return carry

    lax.fori_loop(0, rt // RET_BLOCK, chunk, 0)


def _retention_call(P, dmask, xi_b, zeta_b, beta, decay, *, B, L, rt):
    T = B * L
    nb = L // rt
    kern = functools.partial(_retention_kernel, rt=rt, decay=decay)
    blk = lambda col: pl.BlockSpec((rt, P_TILE), lambda b, l, col=col: (b * nb + l, col))
    cst = lambda w: pl.BlockSpec((RET_HEADS, RET_BLOCK, w), lambda b, l: (0, 0, 0))
    return pl.pallas_call(
        kern,
        grid=(B, nb),
        in_specs=[blk(0), blk(1), blk(2), blk(3), cst(RET_BLOCK), cst(LANES), cst(LANES),
                  pl.BlockSpec((1, RET_W), lambda b, l: (0, 0))],
        out_specs=pl.BlockSpec((rt, RET_W), lambda b, l: (b * nb + l, 0)),
        out_shape=jax.ShapeDtypeStruct((T, RET_W), BF16),
        scratch_shapes=[pltpu.VMEM((RET_HEADS, HEAD_DIM, HEAD_DIM), F32)],
        compiler_params=_cparams(("arbitrary", "arbitrary")),
        name="retention",
    )(P, P, P, P, dmask, xi_b, zeta_b, beta)


def _sortable_key(x):
    b = pltpu.bitcast(x, I32)
    return b ^ ((b >> 31) & 0x7FFFFFFF)


def _tree_sum(parts):
    parts = list(parts)
    while len(parts) > 1:
        nxt = [parts[k] + parts[k + 1] for k in range(0, len(parts) - 1, 2)]
        if len(parts) % 2:
            nxt.append(parts[-1])
        parts = nxt
    return parts[0]


def _bit_transpose32(a):
    a = list(a)
    for k, m in ((16, 0x0000FFFF), (8, 0x00FF00FF), (4, 0x0F0F0F0F), (2, 0x33333333),
                 (1, 0x55555555)):
        for i in range(32):
            if i & k == 0:
                t = ((a[i] >> k) ^ a[i + k]) & jnp.uint32(m)
                a[i + k] = a[i + k] ^ t
                a[i] = a[i] ^ (t << k)
    return a


def _dsa_kernel(dq_ref, iq_ref, iw_ref, dk_ref, vt_ref, ika_ref, ikb_ref, beta_ref, o_ref,
                key_ref, plane_ref, eq_ref, sta_ref, stb_ref, lt_ref, qt_ref, acc_ref, m_ref, l_ref,
                *, topk, nblk):
    i = pl.program_id(1)
    n256 = (i + 2) // 2
    npair = (n256 + 1) // 2
    q0 = i * Q_BLOCK

    @pl.when(jnp.logical_and(pl.program_id(0) == 0, i == 0))
    def _():
        plane_ref[...] = jnp.zeros(plane_ref.shape, U32)

    iq = iq_ref[...].astype(F32)
    for p in range(IDX_HEADS // 2):
        lt_ref[:, p * LANES:(p + 1) * LANES] = iq[:, p * LANES:(p + 1) * LANES].T.astype(BF16)
    dq = dq_ref[...].astype(F32)
    for hh in range(DSA_HEADS):
        qt_ref[:, hh * LANES:(hh + 1) * LANES] = dq[:, hh * LANES:(hh + 1) * LANES].T.astype(BF16)
    iwt = iw_ref[...].astype(F32).T
    w_even = jnp.concatenate([iwt[2 * p:2 * p + 1, :] for p in range(IDX_HEADS // 2)], axis=1)
    w_odd = jnp.concatenate([iwt[2 * p + 1:2 * p + 2, :] for p in range(IDX_HEADS // 2)], axis=1)

    qpos = q0 + lax.broadcasted_iota(I32, (KEY_CHUNK, Q_BLOCK), 1)
    krow = lax.broadcasted_iota(I32, (KEY_CHUNK, Q_BLOCK), 0)

    def score_chunk(c, masked):
        k0 = pl.multiple_of(c * KEY_CHUNK, KEY_CHUNK)
        lt = lt_ref[...]
        se = jnp.dot(ika_ref[pl.ds(k0, KEY_CHUNK), :], lt, preferred_element_type=F32)
        so = jnp.dot(ikb_ref[pl.ds(k0, KEY_CHUNK), :], lt, preferred_element_type=F32)
        te = jnp.maximum(se, 0.0) * w_even
        to = jnp.maximum(so, 0.0) * w_odd
        sc = jnp.zeros((KEY_CHUNK, Q_BLOCK), F32)
        for p in range(IDX_HEADS // 2):
            sc = sc + te[:, p * LANES:(p + 1) * LANES]
            sc = sc + to[:, p * LANES:(p + 1) * LANES]
        key = _sortable_key(sc)
        if masked:
            key = jnp.where(krow + k0 <= qpos, key, INT_MIN)
        key_ref[pl.ds(k0, KEY_CHUNK), :] = key
        u = pltpu.bitcast(key ^ INT_MIN, U32)
        planes = _bit_transpose32([u[8 * j:8 * j + 8, :] for j in range(32)])
        for b in range(32):
            plane_ref[b, c] = planes[b]

    def score_body(p, carry):
        score_chunk(2 * p, False)
        score_chunk(2 * p + 1, False)
        return carry

    lax.fori_loop(0, npair - 1, score_body, 0)
    score_chunk(2 * (npair - 1), True)
    score_chunk(2 * (npair - 1) + 1, True)

    blk_id = lax.broadcasted_iota(I32, (nblk, 8, Q_BLOCK), 0)
    eq_ref[...] = jnp.where(blk_id < 2 * npair, jnp.uint32(0xFFFFFFFF), jnp.uint32(0))

    def search_pass(k, carry):
        t_u, cnt_gt = carry
        b = 31 - k
        eq = eq_ref[...]
        t = eq & plane_ref[b]
        pc = lax.population_count(t)
        tot = jnp.sum(_tree_sum([pc[j] for j in range(nblk)]).astype(I32), axis=0, keepdims=True)
        tot = jnp.broadcast_to(tot, (8, Q_BLOCK))
        take = cnt_gt + tot >= topk
        keep_eq = jnp.where(take, jnp.uint32(0), jnp.uint32(0xFFFFFFFF))
        eq_ref[...] = (eq & keep_eq[None]) ^ t
        bit = jnp.left_shift(jnp.int32(1), b)
        return (jnp.where(take, t_u | bit, t_u), jnp.where(take, cnt_gt, cnt_gt + tot))

    t_u, cnt_gt = lax.fori_loop(0, 32, search_pass,
                                (jnp.zeros((8, Q_BLOCK), I32), jnp.zeros((8, Q_BLOCK), I32)))
    thr = jnp.maximum(t_u[0:1, :] ^ INT_MIN, INT_MIN + 1)

    pc_eq = lax.population_count(eq_ref[...])
    n_eq = jnp.sum(_tree_sum([pc_eq[j] for j in range(nblk)]).astype(I32), axis=0, keepdims=True)
    excess = cnt_gt[0:1, :] + n_eq - topk
    has_ties = jnp.max(excess) > 0

    @pl.when(has_ties)
    def _():
        keep = jnp.where(excess > 0, topk - cnt_gt[0:1, :], jnp.int32(2 ** 30))
        rows = lax.broadcasted_iota(I32, (KEY_CHUNK, Q_BLOCK), 0)

        def count_tied_below(q_cut):
            def step(c, cnt):
                k0 = pl.multiple_of(c * KEY_CHUNK, KEY_CHUNK)
                kk = key_ref[pl.ds(k0, KEY_CHUNK), :]
                hit = jnp.where(kk == thr, jnp.where(rows + k0 < q_cut, 1, 0), 0)
                return cnt + jnp.sum(hit, axis=0, keepdims=True)
            return lax.fori_loop(0, 2 * npair, step, jnp.zeros((1, Q_BLOCK), I32))

        def pos_pass(k, q_max):
            cand = q_max | jnp.left_shift(jnp.int32(1), POS_BITS - 1 - k)
            return jnp.where(count_tied_below(cand) < keep, cand, q_max)

        q_max = lax.fori_loop(0, POS_BITS, pos_pass, jnp.zeros((1, Q_BLOCK), I32))

        def drop_step(c, carry):
            k0 = pl.multiple_of(c * KEY_CHUNK, KEY_CHUNK)
            kk = key_ref[pl.ds(k0, KEY_CHUNK), :]
            drop = jnp.where(kk == thr, jnp.where(rows + k0 > q_max, 1, 0), 0)
            key_ref[pl.ds(k0, KEY_CHUNK), :] = jnp.where(drop > 0, INT_MIN, kk)
            return carry

        lax.fori_loop(0, 2 * npair, drop_step, 0)

    acc_ref[...] = jnp.zeros(acc_ref.shape, F32)
    m_ref[...] = jnp.full(m_ref.shape, NEG_BIG, F32)
    l_ref[...] = jnp.zeros(l_ref.shape, F32)
    ones_rows = jnp.ones((16, KEY_CHUNK), BF16)

    def qk_dots(c, st_ref):
        k0 = pl.multiple_of(c * KEY_CHUNK, KEY_CHUNK)
        for g in range(DSA_KV_HEADS):
            kch = dk_ref[pl.ds(k0, KEY_CHUNK), g * LANES:(g + 1) * LANES]
            qt = qt_ref[:, g * DSA_REP * LANES:(g + 1) * DSA_REP * LANES]
            st_ref[g] = jnp.dot(kch, qt, preferred_element_type=F32)

    def softmax_pv(c, st_ref):
        k0 = pl.multiple_of(c * KEY_CHUNK, KEY_CHUNK)
        sel = key_ref[pl.ds(k0, KEY_CHUNK), :] >= thr
        for g in range(DSA_KV_HEADS):
            st = st_ref[g]
            st = jnp.concatenate(
                [jnp.where(sel, st[:, r * LANES:(r + 1) * LANES], NEG_BIG) for r in range(DSA_REP)],
                axis=1)
            m_old = m_ref[g:g + 1, :]
            m_new = jnp.maximum(m_old, jnp.max(st, axis=0, keepdims=True))
            alpha = jnp.exp2(m_old - m_new)
            p = jnp.exp2(st - m_new)
            m_ref[g:g + 1, :] = m_new
            vt = jnp.concatenate([vt_ref[0, c, g * LANES:(g + 1) * LANES, :], ones_rows], axis=0)
            pv = jnp.dot(vt, p.astype(BF16), preferred_element_type=F32)
            acc_ref[g] = acc_ref[g] * alpha + pv[:HEAD_DIM]
            l_ref[g:g + 1, :] = alpha * l_ref[g:g + 1, :] + pv[HEAD_DIM:HEAD_DIM + 1]

    def attn_body(p, carry):
        c = 2 * p
        qk_dots(c + 1, stb_ref)
        softmax_pv(c, sta_ref)
        qk_dots(c + 2, sta_ref)
        softmax_pv(c + 1, stb_ref)
        return carry

    qk_dots(0, sta_ref)
    lax.fori_loop(0, npair - 1, attn_body, 0)
    c_last = 2 * (npair - 1)
    qk_dots(c_last + 1, stb_ref)
    softmax_pv(c_last, sta_ref)
    softmax_pv(c_last + 1, stb_ref)

    beta = beta_ref[...]
    for g in range(DSA_KV_HEADS):
        ot = acc_ref[g] / l_ref[g:g + 1, :]
        for r in range(DSA_REP):
            hh = g * DSA_REP + r
            blk = ot[:, r * LANES:(r + 1) * LANES].T
            o_ref[:, hh * LANES:(hh + 1) * LANES] = (
                blk * beta[:, hh * LANES:(hh + 1) * LANES]).astype(BF16)


def _dsa_call(P, vT, beta, *, B, L, topk):
    T = B * L
    nq = L // Q_BLOCK
    nblk = L // KEY_CHUNK
    kern = functools.partial(_dsa_kernel, topk=topk, nblk=nblk)
    qrow = lambda col, width: pl.BlockSpec((Q_BLOCK, width),
                                           lambda b, i, col=col: (b * nq + i, col))
    tail128 = TAIL_COL // LANES
    return pl.pallas_call(
        kern,
        grid=(B, nq),
        in_specs=[
            qrow(4, P_TILE),
            qrow(5, P_TILE),
            qrow(tail128 + 6, LANES),
            pl.BlockSpec((L, DSA_KV_W), lambda b, i: (b, TAIL_COL // DSA_KV_W)),
            pl.BlockSpec((1, L // KEY_CHUNK, KEY_CHUNK, KEY_CHUNK), lambda b, i: (b, 0, 0, 0)),
            pl.BlockSpec((L, LANES), lambda b, i: (b, tail128 + 4)),
            pl.BlockSpec((L, LANES), lambda b, i: (b, tail128 + 5)),
            pl.BlockSpec((1, DSA_W), lambda b, i: (0, 0)),
        ],
        out_specs=pl.BlockSpec((Q_BLOCK, DSA_W), lambda b, i: (b * nq + i, 0)),
        out_shape=jax.ShapeDtypeStruct((T, DSA_W), BF16),
        scratch_shapes=[
            pltpu.VMEM((L, Q_BLOCK), I32),
            pltpu.VMEM((32, nblk, 8, Q_BLOCK), U32),
            pltpu.VMEM((nblk, 8, Q_BLOCK), U32),
            pltpu.VMEM((DSA_KV_HEADS, KEY_CHUNK, DSA_REP * Q_BLOCK), F32),
            pltpu.VMEM((DSA_KV_HEADS, KEY_CHUNK, DSA_REP * Q_BLOCK), F32),
            pltpu.VMEM((LANES, P_TILE), BF16),
            pltpu.VMEM((HEAD_DIM, DSA_W), BF16),
            pltpu.VMEM((DSA_KV_HEADS, HEAD_DIM, DSA_REP * Q_BLOCK), F32),
            pltpu.VMEM((DSA_KV_HEADS, DSA_REP * Q_BLOCK), F32),
            pltpu.VMEM((DSA_KV_HEADS, DSA_REP * Q_BLOCK), F32),
        ],
        compiler_params=_cparams(("arbitrary", "arbitrary")),
        name="dsa",
    )(P, P, P, P, vT, P, P, beta)


def _outproj_kernel(yr_ref, yd_ref, x_ref, mod_ref, g2_ref, w_ref, o_ref, h_ref, *, tm):
    half = tm // 2
    for r in range(2):
        rows = slice(r * half, (r + 1) * half)
        mix = (jnp.dot(yr_ref[rows, :], w_ref[0:RET_W, :], preferred_element_type=F32)
               + jnp.dot(yd_ref[rows, :], w_ref[RET_W:RET_W + DSA_W, :],
                         preferred_element_type=F32))
        x1 = x_ref[rows, :] + mod_ref[0, 2:3, :] * mix
        o_ref[rows, :] = x1
        ms = jnp.mean(x1 * x1, axis=-1, keepdims=True)
        y = (x1 * lax.rsqrt(ms + EPS)) * g2_ref[...]
        h_ref[rows, :] = (y * (1.0 + mod_ref[0, 4:5, :]) + mod_ref[0, 3:4, :]).astype(BF16)


def _outproj_call(y_ret, y_dsa, x2, mod3, g2, w_out, *, B, L, tm):
    T = B * L
    tpb = L // tm
    row = lambda w: pl.BlockSpec((tm, w), lambda i: (i, 0))
    return pl.pallas_call(
        functools.partial(_outproj_kernel, tm=tm),
        grid=(T // tm,),
        in_specs=[
            row(RET_W), row(DSA_W), row(D_MODEL),
            pl.BlockSpec((1, N_MOD, D_MODEL), lambda i: (i // tpb, 0, 0)),
            pl.BlockSpec((1, D_MODEL), lambda i: (0, 0)),
            pl.BlockSpec((RET_W + DSA_W, D_MODEL), lambda i: (0, 0)),
        ],
        out_specs=[row(D_MODEL), row(D_MODEL)],
        out_shape=[jax.ShapeDtypeStruct((T, D_MODEL), F32),
                   jax.ShapeDtypeStruct((T, D_MODEL), BF16)],
        compiler_params=_cparams(("arbitrary",)),
        name="outproj",
    )(y_ret, y_dsa, x2, mod3, g2, w_out)


def _mlp_kernel(h_ref, xs_ref, gate_ref, w1_ref, w2_ref, o_ref, a_ref, *, tm):
    s = pl.program_id(1)

    @pl.when(s < MLP_NA)
    def _():
        hid = jnp.dot(h_ref[...], w1_ref[...], preferred_element_type=F32)
        a = jnp.maximum(hid, 0.0)
        a_ref[s] = (a * a).astype(BF16)

    @pl.when(s >= MLP_NA)
    def _():
        half = tm // 2
        for r in range(2):
            rows = slice(r * half, (r + 1) * half)
            y = None
            for f in range(MLP_NA):
                d = jnp.dot(a_ref[f, rows, :], w2_ref[f * MLP_TF:(f + 1) * MLP_TF, :],
                            preferred_element_type=F32)
                y = d if y is None else y + d
            o_ref[rows, :] = xs_ref[rows, :] + gate_ref[0] * y


def _mlp_call(h2, x1, mod3, w1, w2, *, B, L, tm):
    T = B * L
    tpb = L // tm
    gate2 = mod3[:, 5:6, :]
    kern = functools.partial(_mlp_kernel, tm=tm)
    ncol = lambda s: jnp.maximum(s - MLP_NA, 0)
    return pl.pallas_call(
        kern,
        grid=(T // tm, MLP_NA + MLP_NB),
        in_specs=[
            pl.BlockSpec((tm, D_MODEL), lambda i, s: (i, 0)),
            pl.BlockSpec((tm, MLP_TN), lambda i, s: (i, ncol(s))),
            pl.BlockSpec((1, 1, MLP_TN), lambda i, s: (i // tpb, 0, ncol(s))),
            pl.BlockSpec((D_MODEL, MLP_TF), lambda i, s: (0, jnp.minimum(s, MLP_NA - 1))),
            pl.BlockSpec((D_FF, MLP_TN), lambda i, s: (0, ncol(s))),
        ],
        out_specs=pl.BlockSpec((tm, MLP_TN), lambda i, s: (i, ncol(s))),
        out_shape=jax.ShapeDtypeStruct((T, D_MODEL), F32),
        scratch_shapes=[pltpu.VMEM((MLP_NA, tm, MLP_TF), BF16)],
        compiler_params=_cparams(("arbitrary", "arbitrary")),
        name="mlp",
    )(h2, x1, gate2, w1, w2)


def _deinterleave_heads(w):
    k = w.shape[0]
    return w.reshape(k, RET_HEADS, HEAD_DIM // 2, 2).transpose(0, 1, 3, 2).reshape(k, RET_W)


def _prep_w_in(w):
    o = np.cumsum([0, RET_W, RET_W, RET_W, RET_W, DSA_W, DSA_KV_W, DSA_KV_W,
                   IDX_HEADS * IDX_DIM, IDX_DIM, IDX_HEADS])
    part = lambda n: w[:, o[n]:o[n + 1]]
    z = lambda n: jnp.zeros((w.shape[0], n), w.dtype)
    cols = [_deinterleave_heads(part(0)), _deinterleave_heads(part(1)), part(2), part(3),
            part(4), part(7),
            part(5), part(6), part(8), z(LANES - IDX_DIM), z(LANES),
            part(9), z(LANES - IDX_HEADS), z(LANES)]
    return jnp.concatenate(cols, axis=1).astype(BF16)


def _rot_tables(L):
    pos = jnp.arange(L, dtype=F32)
    angle = 1.0 / (10000.0 ** jnp.linspace(0.0, 1.0, HEAD_DIM // 2, dtype=F32))
    theta = pos[:, None] * angle[None, :]
    s, c = jnp.sin(theta), jnp.cos(theta)
    return jnp.concatenate([c, c], axis=1), jnp.concatenate([-s, s], axis=1)


def _ret_tables():
    log_gamma = jnp.log(1.0 - jnp.exp2(-5.0 - jnp.arange(RET_HEADS, dtype=F32)))
    C = RET_BLOCK
    pos = jnp.arange(C, dtype=F32)
    diff = pos[:, None] - pos[None, :]
    dmask = jnp.where(diff[None] >= 0,
                      jnp.exp(log_gamma[:, None, None] * jnp.maximum(diff, 0.0)[None]), 0.0)
    xi = jnp.exp(log_gamma[:, None] * (pos[None, :] + 1.0))
    zeta = jnp.exp(log_gamma[:, None] * (C - 1.0 - pos[None, :]))
    bc = lambda a: jnp.broadcast_to(a[:, :, None], (RET_HEADS, C, LANES))
    return dmask, bc(xi), bc(zeta)


def _pad_lanes(v, n):
    return jnp.concatenate([v, jnp.zeros((n - v.shape[0],), v.dtype)]).reshape(1, n)


def _layer(x, c, w_ada, b_ada, norm1_g, norm2_g, w_in, ret_beta, q_norm_g, k_norm_g,
           idx_k_norm_g, dsa_beta, w_out, w_mlp1, w_mlp2):
    B, L, _ = x.shape
    assert L % (2 * KEY_CHUNK) == 0 and L <= 2 ** (POS_BITS - 1)
    topk = min(TOPK_MAX, L // 4)
    tm = min(1024, L)
    x2 = x.reshape(B * L, D_MODEL)

    mod3 = _mod_call(c, w_ada, b_ada).reshape(B, N_MOD, D_MODEL)

    cos_t, sin_t = _rot_tables(L)
    P, vT = _inproj_call(x2, mod3, norm1_g.reshape(1, -1), _prep_w_in(w_in), cos_t, sin_t,
                         q_norm_g.reshape(1, -1), k_norm_g.reshape(1, -1),
                         _pad_lanes(idx_k_norm_g, LANES), B=B, L=L, tm=tm)

    dmask, xi_b, zeta_b = _ret_tables()
    decay = tuple(float(v) for v in np.exp(_ret_consts() * np.float32(RET_BLOCK)))
    y_ret = _retention_call(P, dmask, xi_b, zeta_b, ret_beta.reshape(1, -1), decay,
                            B=B, L=L, rt=tm)
    y_dsa = _dsa_call(P, vT, dsa_beta.reshape(1, -1), B=B, L=L, topk=topk)

    x1, h2 = _outproj_call(y_ret, y_dsa, x2, mod3, norm2_g.reshape(1, -1), w_out.astype(BF16),
                           B=B, L=L, tm=min(512, L))
    out = _mlp_call(h2, x1, mod3, w_mlp1.astype(BF16), w_mlp2.astype(BF16), B=B, L=L, tm=tm)
    return out.reshape(B, L, D_MODEL)


def kernel(x, c, w_ada, b_ada, norm1_g, norm2_g, w_in, ret_beta, q_norm_g, k_norm_g,
           idx_k_norm_g, dsa_beta, w_out, w_mlp1, w_mlp2):
    for l in range(w_ada.shape[0]):
        x = _layer(x, c, w_ada[l], b_ada[l], norm1_g[l], norm2_g[l], w_in[l], ret_beta[l],
                   q_norm_g[l], k_norm_g[l], idx_k_norm_g[l], dsa_beta[l], w_out[l],
                   w_mlp1[l], w_mlp2[l])
    return x
```

```python
import functools

import numpy as np
import jax
import jax.numpy as jnp
from jax import lax
from jax.experimental import pallas as pl
from jax.experimental.pallas import tpu as pltpu

F32 = jnp.float32
BF16 = jnp.bfloat16
I32 = jnp.int32
U32 = jnp.uint32

D_MODEL = 2048
HEAD_DIM = 128
RET_HEADS = 8
RET_W = 1024
RET_CHUNK = 128
RET_BLOCK = 256
DSA_HEADS = 8
DSA_KV_HEADS = 2
DSA_REP = DSA_HEADS // DSA_KV_HEADS
DSA_W = 1024
DSA_KV_W = 256
IDX_HEADS = 16
IDX_DIM = 64
TOPK_MAX = 256
Q_BLOCK = 128
D_FF = 4 * D_MODEL
N_MOD = 6
EPS = 1e-6

LANES = 128
KEY_CHUNK = 256
MLP_TF = 1024
MLP_TN = 256
MLP_NA = D_FF // MLP_TF
MLP_NB = D_MODEL // MLP_TN
VMEM_LIMIT = 56 * 1024 * 1024

P_TILE = 1024
P_NTILES = 7
P_WIDTH = P_TILE * P_NTILES
TAIL_COL = 6 * P_TILE

POS_BITS = 14
INT_MIN = -(2 ** 31)
NEG_BIG = -1e30
LOG2E = 1.4426950408889634


def _cparams(sem):
    return pltpu.CompilerParams(dimension_semantics=sem, vmem_limit_bytes=VMEM_LIMIT)


def _mod_kernel(c_ref, w_ref, b_ref, o_ref):
    c = c_ref[...]
    s = c * jax.nn.sigmoid(c)
    o_ref[...] = jnp.dot(s.astype(BF16), w_ref[...].astype(BF16),
                         preferred_element_type=F32) + b_ref[...]


def _mod_call(c, w_ada, b_ada):
    B = c.shape[0]
    n = w_ada.shape[1]
    tn = 1536
    return pl.pallas_call(
        _mod_kernel,
        grid=(n // tn,),
        in_specs=[pl.BlockSpec((B, D_MODEL), lambda j: (0, 0)),
                  pl.BlockSpec((D_MODEL, tn), lambda j: (0, j)),
                  pl.BlockSpec((1, tn), lambda j: (0, j))],
        out_specs=pl.BlockSpec((B, tn), lambda j: (0, j)),
        out_shape=jax.ShapeDtypeStruct((B, n), F32),
        compiler_params=_cparams(("arbitrary",)),
        name="mod",
    )(c, w_ada, b_ada.reshape(1, n))


def _rms_scale(xs, width):
    ms = jnp.sum(xs * xs, axis=-1, keepdims=True) * (1.0 / width)
    return xs * lax.rsqrt(ms + EPS)


def _inproj_kernel(x_ref, mod_ref, g1_ref, w_ref, cos_ref, sin_ref, qg_ref, kg_ref, ikg_ref,
                   p_ref, vt_ref, h_ref, *, tm):
    j = pl.program_id(1)

    @pl.when(j == 0)
    def _():
        x = x_ref[...]
        ms = jnp.mean(x * x, axis=-1, keepdims=True)
        y = (x * lax.rsqrt(ms + EPS)) * g1_ref[...]
        h = y * (1.0 + mod_ref[0, 1:2, :]) + mod_ref[0, 0:1, :]
        h_ref[...] = h.astype(BF16)

    half = P_TILE // 2

    def acc_half(n):
        return jnp.dot(h_ref[...], w_ref[:, n * half:(n + 1) * half], preferred_element_type=F32)

    def rotate(scale):
        cos = cos_ref[...]
        sin = sin_ref[...]
        for n in range(2):
            acc = acc_half(n)
            for hh in range(half // LANES):
                xs = acc[:, hh * LANES:(hh + 1) * LANES]
                r = xs * cos + pltpu.roll(xs, LANES // 2, 1) * sin
                if scale != 1.0:
                    r = r * scale
                c0 = n * half + hh * LANES
                p_ref[:, c0:c0 + LANES] = r.astype(BF16)

    @pl.when(j == 0)
    def _():
        rotate(1.0)

    @pl.when(j == 1)
    def _():
        rotate(HEAD_DIM ** -0.5)

    @pl.when(j == 2)
    def _():
        for n in range(2):
            p_ref[:, n * half:(n + 1) * half] = acc_half(n).astype(BF16)

    @pl.when(j == 3)
    def _():
        for n in range(2):
            acc = acc_half(n)
            p_ref[:, n * half:(n + 1) * half] = (acc * jax.nn.sigmoid(acc)).astype(BF16)

    @pl.when(j == 4)
    def _():
        g = qg_ref[...] * (HEAD_DIM ** -0.5 * LOG2E)
        for n in range(2):
            acc = acc_half(n)
            for hh in range(half // LANES):
                xs = acc[:, hh * LANES:(hh + 1) * LANES]
                c0 = n * half + hh * LANES
                p_ref[:, c0:c0 + LANES] = (_rms_scale(xs, HEAD_DIM) * g).astype(BF16)

    @pl.when(j == 5)
    def _():
        for n in range(2):
            p_ref[:, n * half:(n + 1) * half] = (acc_half(n) * (IDX_DIM ** -0.5)).astype(BF16)

    @pl.when(j == 6)
    def _():
        kg = kg_ref[...]
        acc = acc_half(0)
        for g in range(DSA_KV_HEADS):
            xs = acc[:, g * LANES:(g + 1) * LANES]
            p_ref[:, g * LANES:(g + 1) * LANES] = (_rms_scale(xs, HEAD_DIM) * kg).astype(BF16)
        dv = acc[:, 256:512]
        p_ref[:, 256:512] = dv.astype(BF16)
        for s in range(tm // KEY_CHUNK):
            blk = dv[s * KEY_CHUNK:(s + 1) * KEY_CHUNK, :]
            vt_ref[0, s] = blk.T.astype(BF16)
        acc = acc_half(1)
        ik = acc[:, 0:128]
        ika = _rms_scale(ik, IDX_DIM) * ikg_ref[...]
        p_ref[:, 512:640] = ika.astype(BF16)
        p_ref[:, 640:768] = pltpu.roll(ika, LANES // 2, 1).astype(BF16)
        p_ref[:, 768:896] = (acc[:, 256:384] * (IDX_HEADS ** -0.5)).astype(BF16)
        p_ref[:, 896:1024] = jnp.zeros((tm, LANES), BF16)


def _inproj_call(x2, mod3, g1, w_p, cos_t, sin_t, qg, kg, ikg, *, B, L, tm):
    T = B * L
    tpb = L // tm
    kern = functools.partial(_inproj_kernel, tm=tm)
    return pl.pallas_call(
        kern,
        grid=(T // tm, P_NTILES),
        in_specs=[
            pl.BlockSpec((tm, D_MODEL), lambda i, j: (i, 0)),
            pl.BlockSpec((1, N_MOD, D_MODEL), lambda i, j: (i // tpb, 0, 0)),
            pl.BlockSpec((1, D_MODEL), lambda i, j: (0, 0)),
            pl.BlockSpec((D_MODEL, P_TILE), lambda i, j: (0, j)),
            pl.BlockSpec((tm, LANES), lambda i, j: (i % tpb, 0)),
            pl.BlockSpec((tm, LANES), lambda i, j: (i % tpb, 0)),
            pl.BlockSpec((1, LANES), lambda i, j: (0, 0)),
            pl.BlockSpec((1, LANES), lambda i, j: (0, 0)),
            pl.BlockSpec((1, LANES), lambda i, j: (0, 0)),
        ],
        out_specs=[
            pl.BlockSpec((tm, P_TILE), lambda i, j: (i, j)),
            pl.BlockSpec((1, tm // KEY_CHUNK, KEY_CHUNK, KEY_CHUNK),
                         lambda i, j: (i // tpb, i % tpb, 0, 0)),
        ],
        out_shape=[jax.ShapeDtypeStruct((T, P_WIDTH), BF16),
                   jax.ShapeDtypeStruct((B, L // KEY_CHUNK, KEY_CHUNK, KEY_CHUNK), BF16)],
        scratch_shapes=[pltpu.VMEM((tm, D_MODEL), BF16)],
        compiler_params=_cparams(("arbitrary", "arbitrary")),
        name="inproj",
    )(x2, mod3, g1, w_p, cos_t, sin_t, qg, kg, ikg)


def _ret_consts():
    h = np.arange(RET_HEADS, dtype=np.float32)
    log_gamma = np.log(np.float32(1.0) - np.exp2(np.float32(-5.0) - h)).astype(np.float32)
    return log_gamma


def _retention_kernel(q_ref, k_ref, v_ref, g_ref, dm_ref, xi_ref, zeta_ref, beta_ref,
                      o_ref, r_ref, *, rt, decay):
    @pl.when(pl.program_id(1) == 0)
    def _():
        r_ref[...] = jnp.zeros(r_ref.shape, F32)

    nt = (((1,), (1,)), ((), ()))

    def chunk(c, carry):
        r0 = pl.multiple_of(c * RET_BLOCK, RET_BLOCK)
        for hh in range(RET_HEADS):
            cs = slice(hh * LANES, (hh + 1) * LANES)
            q = q_ref[pl.ds(r0, RET_BLOCK), cs]
            k = k_ref[pl.ds(r0, RET_BLOCK), cs]
            v = v_ref[pl.ds(r0, RET_BLOCK), cs]
            inner = lax.dot_general(q, k, nt, preferred_element_type=F32) * dm_ref[hh]
            state = r_ref[hh]
            o = (jnp.dot(inner.astype(BF16), v, preferred_element_type=F32)
                 + jnp.dot(q, state.astype(BF16), preferred_element_type=F32) * xi_ref[hh])
            kz = k.astype(F32) * zeta_ref[hh]
            r_ref[hh] = state * decay[hh] + jnp.dot(kz.T.astype(BF16), v,
                                                    preferred_element_type=F32)
            mu = jnp.mean(o, axis=-1, keepdims=True)
            d = o - mu
            var = jnp.mean(d * d, axis=-1, keepdims=True)
            on = d * lax.rsqrt(var + EPS)
            y = g_ref[pl.ds(r0, RET_BLOCK), cs].astype(F32) * (on * beta_ref[:, cs])
            o_ref[pl.ds(r0, RET_BLOCK), cs] = y.astype(BF16)
        return carry

    lax.fori_loop(0, rt // RET_BLOCK, chunk, 0)


def _retention_call(P, dmask, xi_b, zeta_b, beta, decay, *, B, L, rt):
    T = B * L
    nb = L // rt
    kern = functools.partial(_retention_kernel, rt=rt, decay=decay)
    blk = lambda col: pl.BlockSpec((rt, P_TILE), lambda b, l, col=col: (b * nb + l, col))
    cst = lambda w: pl.BlockSpec((RET_HEADS, RET_BLOCK, w), lambda b, l: (0, 0, 0))
    return pl.pallas_call(
        kern,
        grid=(B, nb),
        in_specs=[blk(0), blk(1), blk(2), blk(3), cst(RET_BLOCK), cst(LANES), cst(LANES),
                  pl.BlockSpec((1, RET_W), lambda b, l: (0, 0))],
        out_specs=pl.BlockSpec((rt, RET_W), lambda b, l: (b * nb + l, 0)),
        out_shape=jax.ShapeDtypeStruct((T, RET_W), BF16),
        scratch_shapes=[pltpu.VMEM((RET_HEADS, HEAD_DIM, HEAD_DIM), F32)],
        compiler_params=_cparams(("arbitrary", "arbitrary")),
        name="retention",
    )(P, P, P, P, dmask, xi_b, zeta_b, beta)


def _sortable_key(x):
    b = pltpu.bitcast(x, I32)
    return b ^ ((b >> 31) & 0x7FFFFFFF)


def _tree_sum(parts):
    parts = list(parts)
    while len(parts) > 1:
        nxt = [parts[k] + parts[k + 1] for k in range(0, len(parts) - 1, 2)]
        if len(parts) % 2:
            nxt.append(parts[-1])
        parts = nxt
    return parts[0]


def _bit_transpose32(a):
    a = list(a)
    for k, m in ((16, 0x0000FFFF), (8, 0x00FF00FF), (4, 0x0F0F0F0F), (2, 0x33333333),
                 (1, 0x55555555)):
        for i in range(32):
            if i & k == 0:
                t = ((a[i] >> k) ^ a[i + k]) & jnp.uint32(m)
                a[i + k] = a[i + k] ^ t
                a[i] = a[i] ^ (t << k)
    return a


def _dsa_kernel(dq_ref, iq_ref, iw_ref, iqn_ref, iwn_ref, dk_ref, vt_ref, ika_ref, ikb_ref,
                beta_ref, o_ref,
                key0_ref, key1_ref, plane0_ref, plane1_ref, eq_ref, sta_ref, stb_ref, lt_ref,
                qt_ref, acc_ref, m_ref, l_ref, *, topk, nblk, nq):
    j = pl.program_id(1)

    def npairs(i):
        return ((i + 2) // 2 + 1) // 2

    @pl.when(jnp.logical_and(pl.program_id(0) == 0, j == 0))
    def _():
        plane0_ref[...] = jnp.zeros(plane0_ref.shape, U32)
        plane1_ref[...] = jnp.zeros(plane1_ref.shape, U32)

    krow = lax.broadcasted_iota(I32, (KEY_CHUNK, Q_BLOCK), 0)
    qlane = lax.broadcasted_iota(I32, (KEY_CHUNK, Q_BLOCK), 1)

    def prep_scores(iq_blk, iw_blk):
        iq = iq_blk.astype(F32)
        for p in range(IDX_HEADS // 2):
            lt_ref[:, p * LANES:(p + 1) * LANES] = iq[:, p * LANES:(p + 1) * LANES].T.astype(BF16)
        iwt = iw_blk.astype(F32).T
        w_even = jnp.concatenate([iwt[2 * p:2 * p + 1, :] for p in range(IDX_HEADS // 2)], axis=1)
        w_odd = jnp.concatenate([iwt[2 * p + 1:2 * p + 2, :] for p in range(IDX_HEADS // 2)],
                                axis=1)
        return w_even, w_odd

    def score_chunk(c, q0, w_even, w_odd, key_ref, plane_ref):
        k0 = pl.multiple_of(c * KEY_CHUNK, KEY_CHUNK)
        lt = lt_ref[...]
        se = jnp.dot(ika_ref[pl.ds(k0, KEY_CHUNK), :], lt, preferred_element_type=F32)
        so = jnp.dot(ikb_ref[pl.ds(k0, KEY_CHUNK), :], lt, preferred_element_type=F32)
        te = jnp.maximum(se, 0.0) * w_even
        to = jnp.maximum(so, 0.0) * w_odd
        sc = jnp.zeros((KEY_CHUNK, Q_BLOCK), F32)
        for p in range(IDX_HEADS // 2):
            sc = sc + te[:, p * LANES:(p + 1) * LANES]
            sc = sc + to[:, p * LANES:(p + 1) * LANES]
        key = jnp.where(krow + k0 <= qlane + q0, _sortable_key(sc), INT_MIN)
        key_ref[pl.ds(k0, KEY_CHUNK), :] = key
        u = pltpu.bitcast(key ^ INT_MIN, U32)
        planes = _bit_transpose32([u[8 * r:8 * r + 8, :] for r in range(32)])
        for b in range(32):
            plane_ref[b, c] = planes[b]

    def search(key_ref, plane_ref, npair):
        blk_id = lax.broadcasted_iota(I32, (nblk, 8, Q_BLOCK), 0)
        eq_ref[...] = jnp.where(blk_id < 2 * npair, jnp.uint32(0xFFFFFFFF), jnp.uint32(0))

        def search_pass(k, carry):
            t_u, cnt_gt = carry
            b = 31 - k
            eq = eq_ref[...]
            t = eq & plane_ref[b]
            pc = lax.population_count(t)
            tot = jnp.sum(_tree_sum([pc[r] for r in range(nblk)]).astype(I32), axis=0,
                          keepdims=True)
            tot = jnp.broadcast_to(tot, (8, Q_BLOCK))
            take = cnt_gt + tot >= topk
            keep_eq = jnp.where(take, jnp.uint32(0), jnp.uint32(0xFFFFFFFF))
            eq_ref[...] = (eq & keep_eq[None]) ^ t
            bit = jnp.left_shift(jnp.int32(1), b)
            return (jnp.where(take, t_u | bit, t_u), jnp.where(take, cnt_gt, cnt_gt + tot))

        t_u, cnt_gt = lax.fori_loop(0, 32, search_pass,
                                    (jnp.zeros((8, Q_BLOCK), I32), jnp.zeros((8, Q_BLOCK), I32)))
        thr = jnp.maximum(t_u[0:1, :] ^ INT_MIN, INT_MIN + 1)

        pc_eq = lax.population_count(eq_ref[...])
        n_eq = jnp.sum(_tree_sum([pc_eq[r] for r in range(nblk)]).astype(I32), axis=0,
                       keepdims=True)
        excess = cnt_gt[0:1, :] + n_eq - topk

        @pl.when(jnp.max(excess) > 0)
        def _():
            keep = jnp.where(excess > 0, topk - cnt_gt[0:1, :], jnp.int32(2 ** 30))

            def count_tied_below(q_cut):
                def step(c, cnt):
                    k0 = pl.multiple_of(c * KEY_CHUNK, KEY_CHUNK)
                    kk = key_ref[pl.ds(k0, KEY_CHUNK), :]
                    hit = jnp.where(kk == thr, jnp.where(krow + k0 < q_cut, 1, 0), 0)
                    return cnt + jnp.sum(hit, axis=0, keepdims=True)
                return lax.fori_loop(0, 2 * npair, step, jnp.zeros((1, Q_BLOCK), I32))

            def pos_pass(k, q_max):
                cand = q_max | jnp.left_shift(jnp.int32(1), POS_BITS - 1 - k)
                return jnp.where(count_tied_below(cand) < keep, cand, q_max)

            q_max = lax.fori_loop(0, POS_BITS, pos_pass, jnp.zeros((1, Q_BLOCK), I32))

            def drop_step(c, carry):
                k0 = pl.multiple_of(c * KEY_CHUNK, KEY_CHUNK)
                kk = key_ref[pl.ds(k0, KEY_CHUNK), :]
                drop = jnp.where(kk == thr, jnp.where(krow + k0 > q_max, 1, 0), 0)
                key_ref[pl.ds(k0, KEY_CHUNK), :] = jnp.where(drop > 0, INT_MIN, kk)
                return carry

            lax.fori_loop(0, 2 * npair, drop_step, 0)

        return thr

    ones_rows = jnp.ones((16, KEY_CHUNK), BF16)

    def qk_dots(c, st_ref):
        k0 = pl.multiple_of(c * KEY_CHUNK, KEY_CHUNK)
        for g in range(DSA_KV_HEADS):
            kch = dk_ref[pl.ds(k0, KEY_CHUNK), g * LANES:(g + 1) * LANES]
            qt = qt_ref[:, g * DSA_REP * LANES:(g + 1) * DSA_REP * LANES]
            st_ref[g] = jnp.dot(kch, qt, preferred_element_type=F32)

    def softmax_pv(c, st_ref, key_ref, thr):
        k0 = pl.multiple_of(c * KEY_CHUNK, KEY_CHUNK)
        sel = key_ref[pl.ds(k0, KEY_CHUNK), :] >= thr
        for g in range(DSA_KV_HEADS):
            st = st_ref[g]
            st = jnp.concatenate(
                [jnp.where(sel, st[:, r * LANES:(r + 1) * LANES], NEG_BIG) for r in range(DSA_REP)],
                axis=1)
            m_old = m_ref[g:g + 1, :]
            m_new = jnp.maximum(m_old, jnp.max(st, axis=0, keepdims=True))
            alpha = jnp.exp2(m_old - m_new)
            p = jnp.exp2(st - m_new)
            m_ref[g:g + 1, :] = m_new
            vt = jnp.concatenate([vt_ref[0, c, g * LANES:(g + 1) * LANES, :], ones_rows], axis=0)
            pv = jnp.dot(vt, p.astype(BF16), preferred_element_type=F32)
            acc_ref[g] = acc_ref[g] * alpha + pv[:HEAD_DIM]
            l_ref[g:g + 1, :] = alpha * l_ref[g:g + 1, :] + pv[HEAD_DIM:HEAD_DIM + 1]

    def half_step(i_att, key_a, plane_a, row0, iq_blk, iw_blk, q0_sc, ps, key_s, plane_s):
        pa = npairs(i_att)
        thr = search(key_a, plane_a, pa)

        dq = dq_ref[row0:row0 + Q_BLOCK, :].astype(F32)
        for hh in range(DSA_HEADS):
            qt_ref[:, hh * LANES:(hh + 1) * LANES] = (
                dq[:, hh * LANES:(hh + 1) * LANES].T.astype(BF16))
        w_even, w_odd = prep_scores(iq_blk, iw_blk)
        acc_ref[...] = jnp.zeros(acc_ref.shape, F32)
        m_ref[...] = jnp.full(m_ref.shape, NEG_BIG, F32)
        l_ref[...] = jnp.zeros(l_ref.shape, F32)

        def score_pair(p):
            score_chunk(2 * p, q0_sc, w_even, w_odd, key_s, plane_s)
            score_chunk(2 * p + 1, q0_sc, w_even, w_odd, key_s, plane_s)

        def fused_body(p, carry):
            c = 2 * p
            qk_dots(c + 1, stb_ref)
            softmax_pv(c, sta_ref, key_a, thr)
            qk_dots(c + 2, sta_ref)
            softmax_pv(c + 1, stb_ref, key_a, thr)
            score_pair(p)
            return carry

        qk_dots(0, sta_ref)
        lax.fori_loop(0, pa - 1, fused_body, 0)
        c_last = 2 * (pa - 1)
        qk_dots(c_last + 1, stb_ref)
        softmax_pv(c_last, sta_ref, key_a, thr)
        softmax_pv(c_last + 1, stb_ref, key_a, thr)
        score_pair(pa - 1)

        @pl.when(ps > pa)
        def _():
            score_pair(pa)

        beta = beta_ref[...]
        for g in range(DSA_KV_HEADS):
            ot = acc_ref[g] / l_ref[g:g + 1, :]
            for r in range(DSA_REP):
                hh = g * DSA_REP + r
                blk = ot[:, r * LANES:(r + 1) * LANES].T
                o_ref[row0:row0 + Q_BLOCK, hh * LANES:(hh + 1) * LANES] = (
                    blk * beta[:, hh * LANES:(hh + 1) * LANES]).astype(BF16)

    i0 = 2 * j
    i1 = i0 + 1
    i2 = i0 + 2

    @pl.when(j == 0)
    def _():
        w_even, w_odd = prep_scores(iq_ref[0:Q_BLOCK, :], iw_ref[0:Q_BLOCK, :])
        score_chunk(0, 0, w_even, w_odd, key0_ref, plane0_ref)
        score_chunk(1, 0, w_even, w_odd, key0_ref, plane0_ref)

    half_step(i0, key0_ref, plane0_ref, 0,
              iq_ref[Q_BLOCK:2 * Q_BLOCK, :], iw_ref[Q_BLOCK:2 * Q_BLOCK, :], i1 * Q_BLOCK,
              npairs(i1), key1_ref, plane1_ref)
    has_next = i2 < nq
    half_step(i1, key1_ref, plane1_ref, Q_BLOCK,
              iqn_ref[...], iwn_ref[...], jnp.minimum(i2, nq - 1) * Q_BLOCK,
              jnp.where(has_next, npairs(i2), npairs(i1)), key0_ref, plane0_ref)


def _dsa_call(P, vT, beta, *, B, L, topk):
    T = B * L
    nq = L // Q_BLOCK
    nq2 = nq // 2
    nblk = L // KEY_CHUNK
    kern = functools.partial(_dsa_kernel, topk=topk, nblk=nblk, nq=nq)
    two = 2 * Q_BLOCK
    qrow2 = lambda col, width: pl.BlockSpec((two, width),
                                            lambda b, j, col=col: (b * nq2 + j, col))
    nxt = lambda col, width: pl.BlockSpec(
        (Q_BLOCK, width), lambda b, j, col=col: (b * nq + jnp.minimum(2 * j + 2, nq - 1), col))
    tail128 = TAIL_COL // LANES
    plane = pltpu.VMEM((32, nblk, 8, Q_BLOCK), U32)
    return pl.pallas_call(
        kern,
        grid=(B, nq2),
        in_specs=[
            qrow2(4, P_TILE),
            qrow2(5, P_TILE),
            qrow2(tail128 + 6, LANES),
            nxt(5, P_TILE),
            nxt(tail128 + 6, LANES),
            pl.BlockSpec((L, DSA_KV_W), lambda b, j: (b, TAIL_COL // DSA_KV_W)),
            pl.BlockSpec((1, L // KEY_CHUNK, KEY_CHUNK, KEY_CHUNK), lambda b, j: (b, 0, 0, 0)),
            pl.BlockSpec((L, LANES), lambda b, j: (b, tail128 + 4)),
            pl.BlockSpec((L, LANES), lambda b, j: (b, tail128 + 5)),
            pl.BlockSpec((1, DSA_W), lambda b, j: (0, 0)),
        ],
        out_specs=pl.BlockSpec((two, DSA_W), lambda b, j: (b * nq2 + j, 0)),
        out_shape=jax.ShapeDtypeStruct((T, DSA_W), BF16),
        scratch_shapes=[
            pltpu.VMEM((L, Q_BLOCK), I32), pltpu.VMEM((L, Q_BLOCK), I32),
            plane, plane,
            pltpu.VMEM((nblk, 8, Q_BLOCK), U32),
            pltpu.VMEM((DSA_KV_HEADS, KEY_CHUNK, DSA_REP * Q_BLOCK), F32),
            pltpu.VMEM((DSA_KV_HEADS, KEY_CHUNK, DSA_REP * Q_BLOCK), F32),
            pltpu.VMEM((LANES, P_TILE), BF16),
            pltpu.VMEM((HEAD_DIM, DSA_W), BF16),
            pltpu.VMEM((DSA_KV_HEADS, HEAD_DIM, DSA_REP * Q_BLOCK), F32),
            pltpu.VMEM((DSA_KV_HEADS, DSA_REP * Q_BLOCK), F32),
            pltpu.VMEM((DSA_KV_HEADS, DSA_REP * Q_BLOCK), F32),
        ],
        compiler_params=_cparams(("arbitrary", "arbitrary")),
        name="dsa",
    )(P, P, P, P, P, P, vT, P, P, beta)


def _outproj_kernel(yr_ref, yd_ref, x_ref, mod_ref, g2_ref, w_ref, o_ref, h_ref, *, tm):
    half = tm // 2
    for r in range(2):
        rows = slice(r * half, (r + 1) * half)
        mix = (jnp.dot(yr_ref[rows, :], w_ref[0:RET_W, :], preferred_element_type=F32)
               + jnp.dot(yd_ref[rows, :], w_ref[RET_W:RET_W + DSA_W, :],
                         preferred_element_type=F32))
        x1 = x_ref[rows, :] + mod_ref[0, 2:3, :] * mix
        o_ref[rows, :] = x1
        ms = jnp.mean(x1 * x1, axis=-1, keepdims=True)
        y = (x1 * lax.rsqrt(ms + EPS)) * g2_ref[...]
        h_ref[rows, :] = (y * (1.0 + mod_ref[0, 4:5, :]) + mod_ref[0, 3:4, :]).astype(BF16)


def _outproj_call(y_ret, y_dsa, x2, mod3, g2, w_out, *, B, L, tm):
    T = B * L
    tpb = L // tm
    row = lambda w: pl.BlockSpec((tm, w), lambda i: (i, 0))
    return pl.pallas_call(
        functools.partial(_outproj_kernel, tm=tm),
        grid=(T // tm,),
        in_specs=[
            row(RET_W), row(DSA_W), row(D_MODEL),
            pl.BlockSpec((1, N_MOD, D_MODEL), lambda i: (i // tpb, 0, 0)),
            pl.BlockSpec((1, D_MODEL), lambda i: (0, 0)),
            pl.BlockSpec((RET_W + DSA_W, D_MODEL), lambda i: (0, 0)),
        ],
        out_specs=[row(D_MODEL), row(D_MODEL)],
        out_shape=[jax.ShapeDtypeStruct((T, D_MODEL), F32),
                   jax.ShapeDtypeStruct((T, D_MODEL), BF16)],
        compiler_params=_cparams(("arbitrary",)),
        name="outproj",
    )(y_ret, y_dsa, x2, mod3, g2, w_out)


def _mlp_kernel(h_ref, xs_ref, gate_ref, w1_ref, w2_ref, o_ref, a_ref, *, tm):
    s = pl.program_id(1)

    @pl.when(s < MLP_NA)
    def _():
        hid = jnp.dot(h_ref[...], w1_ref[...], preferred_element_type=F32)
        a = jnp.maximum(hid, 0.0)
        a_ref[s] = (a * a).astype(BF16)

    @pl.when(s >= MLP_NA)
    def _():
        half = tm // 2
        for r in range(2):
            rows = slice(r * half, (r + 1) * half)
            y = None
            for f in range(MLP_NA):
                d = jnp.dot(a_ref[f, rows, :], w2_ref[f * MLP_TF:(f + 1) * MLP_TF, :],
                            preferred_element_type=F32)
                y = d if y is None else y + d
            o_ref[rows, :] = xs_ref[rows, :] + gate_ref[0] * y


def _mlp_call(h2, x1, mod3, w1, w2, *, B, L, tm):
    T = B * L
    tpb = L // tm
    gate2 = mod3[:, 5:6, :]
    kern = functools.partial(_mlp_kernel, tm=tm)
    ncol = lambda s: jnp.maximum(s - MLP_NA, 0)
    return pl.pallas_call(
        kern,
        grid=(T // tm, MLP_NA + MLP_NB),
        in_specs=[
            pl.BlockSpec((tm, D_MODEL), lambda i, s: (i, 0)),
            pl.BlockSpec((tm, MLP_TN), lambda i, s: (i, ncol(s))),
            pl.BlockSpec((1, 1, MLP_TN), lambda i, s: (i // tpb, 0, ncol(s))),
            pl.BlockSpec((D_MODEL, MLP_TF), lambda i, s: (0, jnp.minimum(s, MLP_NA - 1))),
            pl.BlockSpec((D_FF, MLP_TN), lambda i, s: (0, ncol(s))),
        ],
        out_specs=pl.BlockSpec((tm, MLP_TN), lambda i, s: (i, ncol(s))),
        out_shape=jax.ShapeDtypeStruct((T, D_MODEL), F32),
        scratch_shapes=[pltpu.VMEM((MLP_NA, tm, MLP_TF), BF16)],
        compiler_params=_cparams(("arbitrary", "arbitrary")),
        name="mlp",
    )(h2, x1, gate2, w1, w2)


def _deinterleave_heads(w):
    k = w.shape[0]
    return w.reshape(k, RET_HEADS, HEAD_DIM // 2, 2).transpose(0, 1, 3, 2).reshape(k, RET_W)


def _prep_w_in(w):
    o = np.cumsum([0, RET_W, RET_W, RET_W, RET_W, DSA_W, DSA_KV_W, DSA_KV_W,
                   IDX_HEADS * IDX_DIM, IDX_DIM, IDX_HEADS])
    part = lambda n: w[:, o[n]:o[n + 1]]
    z = lambda n: jnp.zeros((w.shape[0], n), w.dtype)
    cols = [_deinterleave_heads(part(0)), _deinterleave_heads(part(1)), part(2), part(3),
            part(4), part(7),
            part(5), part(6), part(8), z(LANES - IDX_DIM), z(LANES),
            part(9), z(LANES - IDX_HEADS), z(LANES)]
    return jnp.concatenate(cols, axis=1).astype(BF16)


def _rot_tables(L):
    pos = jnp.arange(L, dtype=F32)
    angle = 1.0 / (10000.0 ** jnp.linspace(0.0, 1.0, HEAD_DIM // 2, dtype=F32))
    theta = pos[:, None] * angle[None, :]
    s, c = jnp.sin(theta), jnp.cos(theta)
    return jnp.concatenate([c, c], axis=1), jnp.concatenate([-s, s], axis=1)


def _ret_tables():
    log_gamma = jnp.log(1.0 - jnp.exp2(-5.0 - jnp.arange(RET_HEADS, dtype=F32)))
    C = RET_BLOCK
    pos = jnp.arange(C, dtype=F32)
    diff = pos[:, None] - pos[None, :]
    dmask = jnp.where(diff[None] >= 0,
                      jnp.exp(log_gamma[:, None, None] * jnp.maximum(diff, 0.0)[None]), 0.0)
    xi = jnp.exp(log_gamma[:, None] * (pos[None, :] + 1.0))
    zeta = jnp.exp(log_gamma[:, None] * (C - 1.0 - pos[None, :]))
    bc = lambda a: jnp.broadcast_to(a[:, :, None], (RET_HEADS, C, LANES))
    return dmask, bc(xi), bc(zeta)


def _pad_lanes(v, n):
    return jnp.concatenate([v, jnp.zeros((n - v.shape[0],), v.dtype)]).reshape(1, n)


def _layer(x, c, w_ada, b_ada, norm1_g, norm2_g, w_in, ret_beta, q_norm_g, k_norm_g,
           idx_k_norm_g, dsa_beta, w_out, w_mlp1, w_mlp2):
    B, L, _ = x.shape
    assert L % (2 * KEY_CHUNK) == 0 and L <= 2 ** (POS_BITS - 1)
    topk = min(TOPK_MAX, L // 4)
    tm = min(1024, L)
    x2 = x.reshape(B * L, D_MODEL)

    mod3 = _mod_call(c, w_ada, b_ada).reshape(B, N_MOD, D_MODEL)

    cos_t, sin_t = _rot_tables(L)
    P, vT = _inproj_call(x2, mod3, norm1_g.reshape(1, -1), _prep_w_in(w_in), cos_t, sin_t,
                         q_norm_g.reshape(1, -1), k_norm_g.reshape(1, -1),
                         _pad_lanes(idx_k_norm_g, LANES), B=B, L=L, tm=tm)

    dmask, xi_b, zeta_b = _ret_tables()
    decay = tuple(float(v) for v in np.exp(_ret_consts() * np.float32(RET_BLOCK)))
    y_ret = _retention_call(P, dmask, xi_b, zeta_b, ret_beta.reshape(1, -1), decay,
                            B=B, L=L, rt=tm)
    y_dsa = _dsa_call(P, vT, dsa_beta.reshape(1, -1), B=B, L=L, topk=topk)

    x1, h2 = _outproj_call(y_ret, y_dsa, x2, mod3, norm2_g.reshape(1, -1), w_out.astype(BF16),
                           B=B, L=L, tm=min(512, L))
    out = _mlp_call(h2, x1, mod3, w_mlp1.astype(BF16), w_mlp2.astype(BF16), B=B, L=L, tm=tm)
    return out.reshape(B, L, D_MODEL)


def kernel(x, c, w_ada, b_ada, norm1_g, norm2_g, w_in, ret_beta, q_norm_g, k_norm_g,
           idx_k_norm_g, dsa_beta, w_out, w_mlp1, w_mlp2):
    for l in range(w_ada.shape[0]):
        x = _layer(x, c, w_ada[l], b_ada[l], norm1_g[l], norm2_g[l], w_in[l], ret_beta[l],
                   q_norm_g[l], k_norm_g[l], idx_k_norm_g[l], dsa_beta[l], w_out[l],
                   w_mlp1[l], w_mlp2[l])
    return x
```

```python
import functools

import numpy as np
import jax
import jax.numpy as jnp
from jax import lax
from jax.experimental import pallas as pl
from jax.experimental.pallas import tpu as pltpu

F32 = jnp.float32
BF16 = jnp.bfloat16
I32 = jnp.int32
U32 = jnp.uint32

D_MODEL = 2048
HEAD_DIM = 128
RET_HEADS = 8
RET_W = 1024
RET_CHUNK = 128
RET_BLOCK = 256
DSA_HEADS = 8
DSA_KV_HEADS = 2
DSA_REP = DSA_HEADS // DSA_KV_HEADS
DSA_W = 1024
DSA_KV_W = 256
IDX_HEADS = 16
IDX_DIM = 64
TOPK_MAX = 256
Q_BLOCK = 128
D_FF = 4 * D_MODEL
N_MOD = 6
EPS = 1e-6

LANES = 128
KEY_CHUNK = 256
MLP_TF = 1024
MLP_TN = 256
MLP_NA = D_FF // MLP_TF
MLP_NB = D_MODEL // MLP_TN
VMEM_LIMIT = 56 * 1024 * 1024

P_TILE = 1024
P_NTILES = 7
P_WIDTH = P_TILE * P_NTILES
TAIL_COL = 6 * P_TILE

POS_BITS = 14
INT_MIN = -(2 ** 31)
NEG_BIG = -1e30
LOG2E = 1.4426950408889634


def _cparams(sem):
    return pltpu.CompilerParams(dimension_semantics=sem, vmem_limit_bytes=VMEM_LIMIT)


def _mod_kernel(c_ref, w_ref, b_ref, o_ref):
    c = c_ref[...]
    s = c * jax.nn.sigmoid(c)
    o_ref[...] = jnp.dot(s.astype(BF16), w_ref[...].astype(BF16),
                         preferred_element_type=F32) + b_ref[...]


def _mod_call(c, w_ada, b_ada):
    B = c.shape[0]
    n = w_ada.shape[1]
    tn = 1536
    return pl.pallas_call(
        _mod_kernel,
        grid=(n // tn,),
        in_specs=[pl.BlockSpec((B, D_MODEL), lambda j: (0, 0)),
                  pl.BlockSpec((D_MODEL, tn), lambda j: (0, j)),
                  pl.BlockSpec((1, tn), lambda j: (0, j))],
        out_specs=pl.BlockSpec((B, tn), lambda j: (0, j)),
        out_shape=jax.ShapeDtypeStruct((B, n), F32),
        compiler_params=_cparams(("arbitrary",)),
        name="mod",
    )(c, w_ada, b_ada.reshape(1, n))


def _rms_scale(xs, width):
    ms = jnp.sum(xs * xs, axis=-1, keepdims=True) * (1.0 / width)
    return xs * lax.rsqrt(ms + EPS)


def _inproj_kernel(x_ref, mod_ref, g1_ref, w_ref, cos_ref, sin_ref, qg_ref, kg_ref, ikg_ref,
                   p_ref, vt_ref, h_ref, *, tm):
    j = pl.program_id(1)

    @pl.when(j == 0)
    def _():
        x = x_ref[...]
        ms = jnp.mean(x * x, axis=-1, keepdims=True)
        y = (x * lax.rsqrt(ms + EPS)) * g1_ref[...]
        h = y * (1.0 + mod_ref[0, 1:2, :]) + mod_ref[0, 0:1, :]
        h_ref[...] = h.astype(BF16)

    half = P_TILE // 2

    def acc_half(n):
        return jnp.dot(h_ref[...], w_ref[:, n * half:(n + 1) * half], preferred_element_type=F32)

    def rotate(scale):
        cos = cos_ref[...]
        sin = sin_ref[...]
        for n in range(2):
            acc = acc_half(n)
            for hh in range(half // LANES):
                xs = acc[:, hh * LANES:(hh + 1) * LANES]
                r = xs * cos + pltpu.roll(xs, LANES // 2, 1) * sin
                if scale != 1.0:
                    r = r * scale
                c0 = n * half + hh * LANES
                p_ref[:, c0:c0 + LANES] = r.astype(BF16)

    @pl.when(j == 0)
    def _():
        rotate(1.0)

    @pl.when(j == 1)
    def _():
        rotate(HEAD_DIM ** -0.5)

    @pl.when(j == 2)
    def _():
        for n in range(2):
            p_ref[:, n * half:(n + 1) * half] = acc_half(n).astype(BF16)

    @pl.when(j == 3)
    def _():
        for n in range(2):
            acc = acc_half(n)
            p_ref[:, n * half:(n + 1) * half] = (acc * jax.nn.sigmoid(acc)).astype(BF16)

    @pl.when(j == 4)
    def _():
        g = qg_ref[...] * (HEAD_DIM ** -0.5 * LOG2E)
        for n in range(2):
            acc = acc_half(n)
            for hh in range(half // LANES):
                xs = acc[:, hh * LANES:(hh + 1) * LANES]
                c0 = n * half + hh * LANES
                p_ref[:, c0:c0 + LANES] = (_rms_scale(xs, HEAD_DIM) * g).astype(BF16)

    @pl.when(j == 5)
    def _():
        for n in range(2):
            p_ref[:, n * half:(n + 1) * half] = (acc_half(n) * (IDX_DIM ** -0.5)).astype(BF16)

    @pl.when(j == 6)
    def _():
        kg = kg_ref[...]
        acc = acc_half(0)
        for g in range(DSA_KV_HEADS):
            xs = acc[:, g * LANES:(g + 1) * LANES]
            p_ref[:, g * LANES:(g + 1) * LANES] = (_rms_scale(xs, HEAD_DIM) * kg).astype(BF16)
        dv = acc[:, 256:512]
        p_ref[:, 256:512] = dv.astype(BF16)
        for s in range(tm // KEY_CHUNK):
            blk = dv[s * KEY_CHUNK:(s + 1) * KEY_CHUNK, :]
            vt_ref[0, s] = blk.T.astype(BF16)
        acc = acc_half(1)
        ik = acc[:, 0:128]
        ika = _rms_scale(ik, IDX_DIM) * ikg_ref[...]
        p_ref[:, 512:640] = ika.astype(BF16)
        p_ref[:, 640:768] = pltpu.roll(ika, LANES // 2, 1).astype(BF16)
        p_ref[:, 768:896] = (acc[:, 256:384] * (IDX_HEADS ** -0.5)).astype(BF16)
        p_ref[:, 896:1024] = jnp.zeros((tm, LANES), BF16)


def _inproj_call(x2, mod3, g1, w_p, cos_t, sin_t, qg, kg, ikg, *, B, L, tm):
    T = B * L
    tpb = L // tm
    kern = functools.partial(_inproj_kernel, tm=tm)
    return pl.pallas_call(
        kern,
        grid=(T // tm, P_NTILES),
        in_specs=[
            pl.BlockSpec((tm, D_MODEL), lambda i, j: (i, 0)),
            pl.BlockSpec((1, N_MOD, D_MODEL), lambda i, j: (i // tpb, 0, 0)),
            pl.BlockSpec((1, D_MODEL), lambda i, j: (0, 0)),
            pl.BlockSpec((D_MODEL, P_TILE), lambda i, j: (0, j)),
            pl.BlockSpec((tm, LANES), lambda i, j: (i % tpb, 0)),
            pl.BlockSpec((tm, LANES), lambda i, j: (i % tpb, 0)),
            pl.BlockSpec((1, LANES), lambda i, j: (0, 0)),
            pl.BlockSpec((1, LANES), lambda i, j: (0, 0)),
            pl.BlockSpec((1, LANES), lambda i, j: (0, 0)),
        ],
        out_specs=[
            pl.BlockSpec((tm, P_TILE), lambda i, j: (i, j)),
            pl.BlockSpec((1, tm // KEY_CHUNK, KEY_CHUNK, KEY_CHUNK),
                         lambda i, j: (i // tpb, i % tpb, 0, 0)),
        ],
        out_shape=[jax.ShapeDtypeStruct((T, P_WIDTH), BF16),
                   jax.ShapeDtypeStruct((B, L // KEY_CHUNK, KEY_CHUNK, KEY_CHUNK), BF16)],
        scratch_shapes=[pltpu.VMEM((tm, D_MODEL), BF16)],
        compiler_params=_cparams(("arbitrary", "arbitrary")),
        name="inproj",
    )(x2, mod3, g1, w_p, cos_t, sin_t, qg, kg, ikg)


def _ret_consts():
    h = np.arange(RET_HEADS, dtype=np.float32)
    log_gamma = np.log(np.float32(1.0) - np.exp2(np.float32(-5.0) - h)).astype(np.float32)
    return log_gamma


def _retention_kernel(q_ref, k_ref, v_ref, g_ref, dm_ref, xi_ref, zeta_ref, beta_ref,
                      o_ref, r_ref, *, rt, decay):
    @pl.when(pl.program_id(1) == 0)
    def _():
        r_ref[...] = jnp.zeros(r_ref.shape, F32)

    nt = (((1,), (1,)), ((), ()))

    def chunk(c, carry):
        r0 = pl.multiple_of(c * RET_BLOCK, RET_BLOCK)
        for hh in range(RET_HEADS):
            cs = slice(hh * LANES, (hh + 1) * LANES)
            q = q_ref[pl.ds(r0, RET_BLOCK), cs]
            k = k_ref[pl.ds(r0, RET_BLOCK), cs]
            v = v_ref[pl.ds(r0, RET_BLOCK), cs]
            inner = lax.dot_general(q, k, nt, preferred_element_type=F32) * dm_ref[hh]
            state = r_ref[hh]
            o = (jnp.dot(inner.astype(BF16), v, preferred_element_type=F32)
                 + jnp.dot(q, state.astype(BF16), preferred_element_type=F32) * xi_ref[hh])
            kz = k.astype(F32) * zeta_ref[hh]
            r_ref[hh] = state * decay[hh] + jnp.dot(kz.T.astype(BF16), v,
                                                    preferred_element_type=F32)
            mu = jnp.mean(o, axis=-1, keepdims=True)
            d = o - mu
            var = jnp.mean(d * d, axis=-1, keepdims=True)
            on = d * lax.rsqrt(var + EPS)
            y = g_ref[pl.ds(r0, RET_BLOCK), cs].astype(F32) * (on * beta_ref[:, cs])
            o_ref[pl.ds(r0, RET_BLOCK), cs] = y.astype(BF16)
        return carry

    lax.fori_loop(0, rt // RET_BLOCK, chunk, 0)


def _retention_call(P, dmask, xi_b, zeta_b, beta, decay, *, B, L, rt):
    T = B * L
    nb = L // rt
    kern = functools.partial(_retention_kernel, rt=rt, decay=decay)
    blk = lambda col: pl.BlockSpec((rt, P_TILE), lambda b, l, col=col: (b * nb + l, col))
    cst = lambda w: pl.BlockSpec((RET_HEADS, RET_BLOCK, w), lambda b, l: (0, 0, 0))
    return pl.pallas_call(
        kern,
        grid=(B, nb),
        in_specs=[blk(0), blk(1), blk(2), blk(3), cst(RET_BLOCK), cst(LANES), cst(LANES),
                  pl.BlockSpec((1, RET_W), lambda b, l: (0, 0))],
        out_specs=pl.BlockSpec((rt, RET_W), lambda b, l: (b * nb + l, 0)),
        out_shape=jax.ShapeDtypeStruct((T, RET_W), BF16),
        scratch_shapes=[pltpu.VMEM((RET_HEADS, HEAD_DIM, HEAD_DIM), F32)],
        compiler_params=_cparams(("arbitrary", "arbitrary")),
        name="retention",
    )(P, P, P, P, dmask, xi_b, zeta_b, beta)


def _sortable_key(x):
    b = pltpu.bitcast(x, I32)
    return b ^ ((b >> 31) & 0x7FFFFFFF)


def _tree_sum(parts):
    parts = list(parts)
    while len(parts) > 1:
        nxt = [parts[k] + parts[k + 1] for k in range(0, len(parts) - 1, 2)]
        if len(parts) % 2:
            nxt.append(parts[-1])
        parts = nxt
    return parts[0]


def _bit_transpose32(a):
    a = list(a)
    for k, m in ((16, 0x0000FFFF), (8, 0x00FF00FF), (4, 0x0F0F0F0F), (2, 0x33333333),
                 (1, 0x55555555)):
        for i in range(32):
            if i & k == 0:
                t = ((a[i] >> k) ^ a[i + k]) & jnp.uint32(m)
                a[i + k] = a[i + k] ^ t
                a[i] = a[i] ^ (t << k)
    return a


def _dsa_kernel(dq_ref, iq_ref, iw_ref, iqn_ref, iwn_ref, dk_ref, vt_ref, ika_ref, ikb_ref,
                beta_ref, o_ref,
                key0_ref, key1_ref, plane0_ref, plane1_ref, eq_ref, sta_ref, stb_ref, lt_ref,
                qt_ref, acc_ref, m_ref, l_ref, *, topk, nblk, nq):
    j = pl.program_id(1)

    def npairs(i):
        return ((i + 2) // 2 + 1) // 2

    @pl.when(jnp.logical_and(pl.program_id(0) == 0, j == 0))
    def _():
        plane0_ref[...] = jnp.zeros(plane0_ref.shape, U32)
        plane1_ref[...] = jnp.zeros(plane1_ref.shape, U32)

    krow = lax.broadcasted_iota(I32, (KEY_CHUNK, Q_BLOCK), 0)
    qlane = lax.broadcasted_iota(I32, (KEY_CHUNK, Q_BLOCK), 1)

    def prep_scores(iq_blk, iw_blk):
        iq = iq_blk.astype(F32)
        for p in range(IDX_HEADS // 2):
            lt_ref[:, p * LANES:(p + 1) * LANES] = iq[:, p * LANES:(p + 1) * LANES].T.astype(BF16)
        iwt = iw_blk.astype(F32).T
        w_even = jnp.concatenate([iwt[2 * p:2 * p + 1, :] for p in range(IDX_HEADS // 2)], axis=1)
        w_odd = jnp.concatenate([iwt[2 * p + 1:2 * p + 2, :] for p in range(IDX_HEADS // 2)],
                                axis=1)
        return w_even, w_odd

    def score_chunk(c, q0, w_even, w_odd, key_ref, plane_ref, masked=True):
        k0 = pl.multiple_of(c * KEY_CHUNK, KEY_CHUNK)
        lt = lt_ref[...]
        se = jnp.dot(ika_ref[pl.ds(k0, KEY_CHUNK), :], lt, preferred_element_type=F32)
        so = jnp.dot(ikb_ref[pl.ds(k0, KEY_CHUNK), :], lt, preferred_element_type=F32)
        te = jnp.maximum(se, 0.0) * w_even
        to = jnp.maximum(so, 0.0) * w_odd
        sc = jnp.zeros((KEY_CHUNK, Q_BLOCK), F32)
        for p in range(IDX_HEADS // 2):
            sc = sc + te[:, p * LANES:(p + 1) * LANES]
            sc = sc + to[:, p * LANES:(p + 1) * LANES]
        key = _sortable_key(sc)
        if masked:
            key = jnp.where(krow + k0 <= qlane + q0, key, INT_MIN)
        key_ref[pl.ds(k0, KEY_CHUNK), :] = key
        u = pltpu.bitcast(key ^ INT_MIN, U32)
        planes = _bit_transpose32([u[8 * r:8 * r + 8, :] for r in range(32)])
        for b in range(32):
            plane_ref[b, c] = planes[b]

    def search(key_ref, plane_ref, npair):
        blk_id = lax.broadcasted_iota(I32, (nblk, 8, Q_BLOCK), 0)
        eq_ref[...] = jnp.where(blk_id < 2 * npair, jnp.uint32(0xFFFFFFFF), jnp.uint32(0))

        def make_pass(nb):
            def search_pass(k, carry):
                t_u, cnt_gt = carry
                b = 31 - k
                eq = eq_ref[0:nb]
                t = eq & plane_ref[b, 0:nb]
                pc = lax.population_count(t)
                tot = jnp.sum(_tree_sum([pc[r] for r in range(nb)]).astype(I32), axis=0,
                              keepdims=True)
                tot = jnp.broadcast_to(tot, (8, Q_BLOCK))
                take = cnt_gt + tot >= topk
                keep_eq = jnp.where(take, jnp.uint32(0), jnp.uint32(0xFFFFFFFF))
                eq_ref[0:nb] = (eq & keep_eq[None]) ^ t
                bit = jnp.left_shift(jnp.int32(1), b)
                return (jnp.where(take, t_u | bit, t_u), jnp.where(take, cnt_gt, cnt_gt + tot))
            return search_pass

        zero = (jnp.zeros((8, Q_BLOCK), I32), jnp.zeros((8, Q_BLOCK), I32))
        t_u, cnt_gt = lax.cond(
            2 * npair <= nblk // 2,
            lambda: lax.fori_loop(0, 32, make_pass(nblk // 2), zero),
            lambda: lax.fori_loop(0, 32, make_pass(nblk), zero))
        thr = jnp.maximum(t_u[0:1, :] ^ INT_MIN, INT_MIN + 1)

        pc_eq = lax.population_count(eq_ref[...])
        n_eq = jnp.sum(_tree_sum([pc_eq[r] for r in range(nblk)]).astype(I32), axis=0,
                       keepdims=True)
        excess = cnt_gt[0:1, :] + n_eq - topk

        @pl.when(jnp.max(excess) > 0)
        def _():
            keep = jnp.where(excess > 0, topk - cnt_gt[0:1, :], jnp.int32(2 ** 30))

            def count_tied_below(q_cut):
                def step(c, cnt):
                    k0 = pl.multiple_of(c * KEY_CHUNK, KEY_CHUNK)
                    kk = key_ref[pl.ds(k0, KEY_CHUNK), :]
                    hit = jnp.where(kk == thr, jnp.where(krow + k0 < q_cut, 1, 0), 0)
                    return cnt + jnp.sum(hit, axis=0, keepdims=True)
                return lax.fori_loop(0, 2 * npair, step, jnp.zeros((1, Q_BLOCK), I32))

            def pos_pass(k, q_max):
                cand = q_max | jnp.left_shift(jnp.int32(1), POS_BITS - 1 - k)
                return jnp.where(count_tied_below(cand) < keep, cand, q_max)

            q_max = lax.fori_loop(0, POS_BITS, pos_pass, jnp.zeros((1, Q_BLOCK), I32))

            def drop_step(c, carry):
                k0 = pl.multiple_of(c * KEY_CHUNK, KEY_CHUNK)
                kk = key_ref[pl.ds(k0, KEY_CHUNK), :]
                drop = jnp.where(kk == thr, jnp.where(krow + k0 > q_max, 1, 0), 0)
                key_ref[pl.ds(k0, KEY_CHUNK), :] = jnp.where(drop > 0, INT_MIN, kk)
                return carry

            lax.fori_loop(0, 2 * npair, drop_step, 0)

        return thr

    ones_rows = jnp.ones((16, KEY_CHUNK), BF16)

    def qk_dots(c, st_ref):
        k0 = pl.multiple_of(c * KEY_CHUNK, KEY_CHUNK)
        for g in range(DSA_KV_HEADS):
            kch = dk_ref[pl.ds(k0, KEY_CHUNK), g * LANES:(g + 1) * LANES]
            qt = qt_ref[:, g * DSA_REP * LANES:(g + 1) * DSA_REP * LANES]
            st_ref[g] = jnp.dot(kch, qt, preferred_element_type=F32)

    def softmax_pv(c, st_ref, key_ref, thr):
        k0 = pl.multiple_of(c * KEY_CHUNK, KEY_CHUNK)
        sel = key_ref[pl.ds(k0, KEY_CHUNK), :] >= thr
        for g in range(DSA_KV_HEADS):
            st = st_ref[g]
            st = jnp.concatenate(
                [jnp.where(sel, st[:, r * LANES:(r + 1) * LANES], NEG_BIG) for r in range(DSA_REP)],
                axis=1)
            m_old = m_ref[g:g + 1, :]
            m_new = jnp.maximum(m_old, jnp.max(st, axis=0, keepdims=True))
            alpha = jnp.exp2(m_old - m_new)
            p = jnp.exp2(st - m_new)
            m_ref[g:g + 1, :] = m_new
            vt = jnp.concatenate([vt_ref[0, c, g * LANES:(g + 1) * LANES, :], ones_rows], axis=0)
            pv = jnp.dot(vt, p.astype(BF16), preferred_element_type=F32)
            acc_ref[g] = acc_ref[g] * alpha + pv[:HEAD_DIM]
            l_ref[g:g + 1, :] = alpha * l_ref[g:g + 1, :] + pv[HEAD_DIM:HEAD_DIM + 1]

    def half_step(i_att, key_a, plane_a, row0, iq_blk, iw_blk, q0_sc, ps, key_s, plane_s):
        pa = npairs(i_att)
        thr = search(key_a, plane_a, pa)

        dq = dq_ref[row0:row0 + Q_BLOCK, :].astype(F32)
        for hh in range(DSA_HEADS):
            qt_ref[:, hh * LANES:(hh + 1) * LANES] = (
                dq[:, hh * LANES:(hh + 1) * LANES].T.astype(BF16))
        w_even, w_odd = prep_scores(iq_blk, iw_blk)
        acc_ref[...] = jnp.zeros(acc_ref.shape, F32)
        m_ref[...] = jnp.full(m_ref.shape, NEG_BIG, F32)
        l_ref[...] = jnp.zeros(l_ref.shape, F32)

        def score_pair(p, masked=True):
            score_chunk(2 * p, q0_sc, w_even, w_odd, key_s, plane_s, masked)
            score_chunk(2 * p + 1, q0_sc, w_even, w_odd, key_s, plane_s, masked)

        def fused_body(p, carry):
            c = 2 * p
            qk_dots(c + 1, stb_ref)
            softmax_pv(c, sta_ref, key_a, thr)
            qk_dots(c + 2, sta_ref)
            softmax_pv(c + 1, stb_ref, key_a, thr)
            score_pair(p, masked=False)
            return carry

        qk_dots(0, sta_ref)
        lax.fori_loop(0, pa - 1, fused_body, 0)
        c_last = 2 * (pa - 1)
        qk_dots(c_last + 1, stb_ref)
        softmax_pv(c_last, sta_ref, key_a, thr)
        softmax_pv(c_last + 1, stb_ref, key_a, thr)
        score_pair(pa - 1)

        @pl.when(ps > pa)
        def _():
            score_pair(pa)

        beta = beta_ref[...]
        for g in range(DSA_KV_HEADS):
            ot = acc_ref[g] / l_ref[g:g + 1, :]
            for r in range(DSA_REP):
                hh = g * DSA_REP + r
                blk = ot[:, r * LANES:(r + 1) * LANES].T
                o_ref[row0:row0 + Q_BLOCK, hh * LANES:(hh + 1) * LANES] = (
                    blk * beta[:, hh * LANES:(hh + 1) * LANES]).astype(BF16)

    i0 = 2 * j
    i1 = i0 + 1
    i2 = i0 + 2

    @pl.when(j == 0)
    def _():
        w_even, w_odd = prep_scores(iq_ref[0:Q_BLOCK, :], iw_ref[0:Q_BLOCK, :])
        score_chunk(0, 0, w_even, w_odd, key0_ref, plane0_ref)
        score_chunk(1, 0, w_even, w_odd, key0_ref, plane0_ref)

    half_step(i0, key0_ref, plane0_ref, 0,
              iq_ref[Q_BLOCK:2 * Q_BLOCK, :], iw_ref[Q_BLOCK:2 * Q_BLOCK, :], i1 * Q_BLOCK,
              npairs(i1), key1_ref, plane1_ref)
    has_next = i2 < nq
    half_step(i1, key1_ref, plane1_ref, Q_BLOCK,
              iqn_ref[...], iwn_ref[...], jnp.minimum(i2, nq - 1) * Q_BLOCK,
              jnp.where(has_next, npairs(i2), npairs(i1)), key0_ref, plane0_ref)


def _dsa_call(P, vT, beta, *, B, L, topk):
    T = B * L
    nq = L // Q_BLOCK
    nq2 = nq // 2
    nblk = L // KEY_CHUNK
    kern = functools.partial(_dsa_kernel, topk=topk, nblk=nblk, nq=nq)
    two = 2 * Q_BLOCK
    qrow2 = lambda col, width: pl.BlockSpec((two, width),
                                            lambda b, j, col=col: (b * nq2 + j, col))
    nxt = lambda col, width: pl.BlockSpec(
        (Q_BLOCK, width), lambda b, j, col=col: (b * nq + jnp.minimum(2 * j + 2, nq - 1), col))
    tail128 = TAIL_COL // LANES
    plane = pltpu.VMEM((32, nblk, 8, Q_BLOCK), U32)
    return pl.pallas_call(
        kern,
        grid=(B, nq2),
        in_specs=[
            qrow2(4, P_TILE),
            qrow2(5, P_TILE),
            qrow2(tail128 + 6, LANES),
            nxt(5, P_TILE),
            nxt(tail128 + 6, LANES),
            pl.BlockSpec((L, DSA_KV_W), lambda b, j: (b, TAIL_COL // DSA_KV_W)),
            pl.BlockSpec((1, L // KEY_CHUNK, KEY_CHUNK, KEY_CHUNK), lambda b, j: (b, 0, 0, 0)),
            pl.BlockSpec((L, LANES), lambda b, j: (b, tail128 + 4)),
            pl.BlockSpec((L, LANES), lambda b, j: (b, tail128 + 5)),
            pl.BlockSpec((1, DSA_W), lambda b, j: (0, 0)),
        ],
        out_specs=pl.BlockSpec((two, DSA_W), lambda b, j: (b * nq2 + j, 0)),
        out_shape=jax.ShapeDtypeStruct((T, DSA_W), BF16),
        scratch_shapes=[
            pltpu.VMEM((L, Q_BLOCK), I32), pltpu.VMEM((L, Q_BLOCK), I32),
            plane, plane,
            pltpu.VMEM((nblk, 8, Q_BLOCK), U32),
            pltpu.VMEM((DSA_KV_HEADS, KEY_CHUNK, DSA_REP * Q_BLOCK), F32),
            pltpu.VMEM((DSA_KV_HEADS, KEY_CHUNK, DSA_REP * Q_BLOCK), F32),
            pltpu.VMEM((LANES, P_TILE), BF16),
            pltpu.VMEM((HEAD_DIM, DSA_W), BF16),
            pltpu.VMEM((DSA_KV_HEADS, HEAD_DIM, DSA_REP * Q_BLOCK), F32),
            pltpu.VMEM((DSA_KV_HEADS, DSA_REP * Q_BLOCK), F32),
            pltpu.VMEM((DSA_KV_HEADS, DSA_REP * Q_BLOCK), F32),
        ],
        compiler_params=_cparams(("arbitrary", "arbitrary")),
        name="dsa",
    )(P, P, P, P, P, P, vT, P, P, beta)


def _outproj_kernel(yr_ref, yd_ref, x_ref, mod_ref, g2_ref, w_ref, o_ref, h_ref, *, tm):
    half = tm // 2
    for r in range(2):
        rows = slice(r * half, (r + 1) * half)
        mix = (jnp.dot(yr_ref[rows, :], w_ref[0:RET_W, :], preferred_element_type=F32)
               + jnp.dot(yd_ref[rows, :], w_ref[RET_W:RET_W + DSA_W, :],
                         preferred_element_type=F32))
        x1 = x_ref[rows, :] + mod_ref[0, 2:3, :] * mix
        o_ref[rows, :] = x1
        ms = jnp.mean(x1 * x1, axis=-1, keepdims=True)
        y = (x1 * lax.rsqrt(ms + EPS)) * g2_ref[...]
        h_ref[rows, :] = (y * (1.0 + mod_ref[0, 4:5, :]) + mod_ref[0, 3:4, :]).astype(BF16)


def _outproj_call(y_ret, y_dsa, x2, mod3, g2, w_out, *, B, L, tm):
    T = B * L
    tpb = L // tm
    row = lambda w: pl.BlockSpec((tm, w), lambda i: (i, 0))
    return pl.pallas_call(
        functools.partial(_outproj_kernel, tm=tm),
        grid=(T // tm,),
        in_specs=[
            row(RET_W), row(DSA_W), row(D_MODEL),
            pl.BlockSpec((1, N_MOD, D_MODEL), lambda i: (i // tpb, 0, 0)),
            pl.BlockSpec((1, D_MODEL), lambda i: (0, 0)),
            pl.BlockSpec((RET_W + DSA_W, D_MODEL), lambda i: (0, 0)),
        ],
        out_specs=[row(D_MODEL), row(D_MODEL)],
        out_shape=[jax.ShapeDtypeStruct((T, D_MODEL), F32),
                   jax.ShapeDtypeStruct((T, D_MODEL), BF16)],
        compiler_params=_cparams(("arbitrary",)),
        name="outproj",
    )(y_ret, y_dsa, x2, mod3, g2, w_out)


def _mlp_kernel(h_ref, xs_ref, gate_ref, w1_ref, w2_ref, o_ref, a_ref, *, tm):
    s = pl.program_id(1)

    @pl.when(s < MLP_NA)
    def _():
        hid = jnp.dot(h_ref[...], w1_ref[...], preferred_element_type=F32)
        a = jnp.maximum(hid, 0.0)
        a_ref[s] = (a * a).astype(BF16)

    @pl.when(s >= MLP_NA)
    def _():
        half = tm // 2
        for r in range(2):
            rows = slice(r * half, (r + 1) * half)
            y = None
            for f in range(MLP_NA):
                d = jnp.dot(a_ref[f, rows, :], w2_ref[f * MLP_TF:(f + 1) * MLP_TF, :],
                            preferred_element_type=F32)
                y = d if y is None else y + d
            o_ref[rows, :] = xs_ref[rows, :] + gate_ref[0] * y


def _mlp_call(h2, x1, mod3, w1, w2, *, B, L, tm):
    T = B * L
    tpb = L // tm
    gate2 = mod3[:, 5:6, :]
    kern = functools.partial(_mlp_kernel, tm=tm)
    ncol = lambda s: jnp.maximum(s - MLP_NA, 0)
    return pl.pallas_call(
        kern,
        grid=(T // tm, MLP_NA + MLP_NB),
        in_specs=[
            pl.BlockSpec((tm, D_MODEL), lambda i, s: (i, 0)),
            pl.BlockSpec((tm, MLP_TN), lambda i, s: (i, ncol(s))),
            pl.BlockSpec((1, 1, MLP_TN), lambda i, s: (i // tpb, 0, ncol(s))),
            pl.BlockSpec((D_MODEL, MLP_TF), lambda i, s: (0, jnp.minimum(s, MLP_NA - 1))),
            pl.BlockSpec((D_FF, MLP_TN), lambda i, s: (0, ncol(s))),
        ],
        out_specs=pl.BlockSpec((tm, MLP_TN), lambda i, s: (i, ncol(s))),
        out_shape=jax.ShapeDtypeStruct((T, D_MODEL), F32),
        scratch_shapes=[pltpu.VMEM((MLP_NA, tm, MLP_TF), BF16)],
        compiler_params=_cparams(("arbitrary", "arbitrary")),
        name="mlp",
    )(h2, x1, gate2, w1, w2)


def _deinterleave_heads(w):
    k = w.shape[0]
    return w.reshape(k, RET_HEADS, HEAD_DIM // 2, 2).transpose(0, 1, 3, 2).reshape(k, RET_W)


def _prep_w_in(w):
    o = np.cumsum([0, RET_W, RET_W, RET_W, RET_W, DSA_W, DSA_KV_W, DSA_KV_W,
                   IDX_HEADS * IDX_DIM, IDX_DIM, IDX_HEADS])
    part = lambda n: w[:, o[n]:o[n + 1]]
    z = lambda n: jnp.zeros((w.shape[0], n), w.dtype)
    cols = [_deinterleave_heads(part(0)), _deinterleave_heads(part(1)), part(2), part(3),
            part(4), part(7),
            part(5), part(6), part(8), z(LANES - IDX_DIM), z(LANES),
            part(9), z(LANES - IDX_HEADS), z(LANES)]
    return jnp.concatenate(cols, axis=1).astype(BF16)


def _rot_tables(L):
    pos = np.arange(L, dtype=np.float32)
    angle = (np.float32(1.0) / np.float32(10000.0) ** np.linspace(0.0, 1.0, HEAD_DIM // 2,
                                                                  dtype=np.float32)).astype(np.float32)
    theta = (pos[:, None] * angle[None, :]).astype(np.float32)
    s, c = np.sin(theta).astype(np.float32), np.cos(theta).astype(np.float32)
    return np.concatenate([c, c], axis=1), np.concatenate([-s, s], axis=1)


def _ret_tables():
    log_gamma = _ret_consts()
    C = RET_BLOCK
    pos = np.arange(C, dtype=np.float32)
    diff = pos[:, None] - pos[None, :]
    dmask = np.where(diff[None] >= 0,
                     np.exp(log_gamma[:, None, None] * np.maximum(diff, 0.0)[None]), 0.0)
    xi = np.exp(log_gamma[:, None] * (pos[None, :] + np.float32(1.0)))
    zeta = np.exp(log_gamma[:, None] * (np.float32(C - 1.0) - pos[None, :]))
    bc = lambda a: np.ascontiguousarray(
        np.broadcast_to(a.astype(np.float32)[:, :, None], (RET_HEADS, C, LANES)))
    return dmask.astype(np.float32), bc(xi), bc(zeta)


def _pad_lanes(v, n):
    return jnp.concatenate([v, jnp.zeros((n - v.shape[0],), v.dtype)]).reshape(1, n)


def _layer(x, c, w_ada, b_ada, norm1_g, norm2_g, w_in, ret_beta, q_norm_g, k_norm_g,
           idx_k_norm_g, dsa_beta, w_out, w_mlp1, w_mlp2):
    B, L, _ = x.shape
    assert L % (2 * KEY_CHUNK) == 0 and L <= 2 ** (POS_BITS - 1)
    topk = min(TOPK_MAX, L // 4)
    tm = min(1024, L)
    x2 = x.reshape(B * L, D_MODEL)

    mod3 = _mod_call(c, w_ada, b_ada).reshape(B, N_MOD, D_MODEL)

    cos_t, sin_t = _rot_tables(L)
    P, vT = _inproj_call(x2, mod3, norm1_g.reshape(1, -1), _prep_w_in(w_in), cos_t, sin_t,
                         q_norm_g.reshape(1, -1), k_norm_g.reshape(1, -1),
                         _pad_lanes(idx_k_norm_g, LANES), B=B, L=L, tm=tm)

    dmask, xi_b, zeta_b = _ret_tables()
    decay = tuple(float(v) for v in np.exp(_ret_consts() * np.float32(RET_BLOCK)))
    y_ret = _retention_call(P, dmask, xi_b, zeta_b, ret_beta.reshape(1, -1), decay,
                            B=B, L=L, rt=tm)
    y_dsa = _dsa_call(P, vT, dsa_beta.reshape(1, -1), B=B, L=L, topk=topk)

    x1, h2 = _outproj_call(y_ret, y_dsa, x2, mod3, norm2_g.reshape(1, -1), w_out.astype(BF16),
                           B=B, L=L, tm=min(512, L))
    out = _mlp_call(h2, x1, mod3, w_mlp1.astype(BF16), w_mlp2.astype(BF16), B=B, L=L, tm=tm)
    return out.reshape(B, L, D_MODEL)


def kernel(x, c, w_ada, b_ada, norm1_g, norm2_g, w_in, ret_beta, q_norm_g, k_norm_g,
           idx_k_norm_g, dsa_beta, w_out, w_mlp1, w_mlp2):
    for l in range(w_ada.shape[0]):
        x = _layer(x, c, w_ada[l], b_ada[l], norm1_g[l], norm2_g[l], w_in[l], ret_beta[l],
                   q_norm_g[l], k_norm_g[l], idx_k_norm_g[l], dsa_beta[l], w_out[l],
                   w_mlp1[l], w_mlp2[l])
    return x
```

```python
import functools

import numpy as np
import jax
import jax.numpy as jnp
from jax import lax
from jax.experimental import pallas as pl
from jax.experimental.pallas import tpu as pltpu

F32 = jnp.float32
BF16 = jnp.bfloat16
I32 = jnp.int32
U32 = jnp.uint32

D_MODEL = 2048
HEAD_DIM = 128
RET_HEADS = 8
RET_W = 1024
RET_CHUNK = 128
RET_BLOCK = 256
DSA_HEADS = 8
DSA_KV_HEADS = 2
DSA_REP = DSA_HEADS // DSA_KV_HEADS
DSA_W = 1024
DSA_KV_W = 256
IDX_HEADS = 16
IDX_DIM = 64
TOPK_MAX = 256
Q_BLOCK = 128
D_FF = 4 * D_MODEL
N_MOD = 6
EPS = 1e-6

LANES = 128
KEY_CHUNK = 256
MLP_TF = 1024
MLP_TN = 256
MLP_NA = D_FF // MLP_TF
MLP_NB = D_MODEL // MLP_TN
VMEM_LIMIT = 56 * 1024 * 1024

P_TILE = 1024
P_NTILES = 7
P_WIDTH = P_TILE * P_NTILES
TAIL_COL = 6 * P_TILE

POS_BITS = 14
INT_MIN = -(2 ** 31)
NEG_BIG = -1e30
LOG2E = 1.4426950408889634


def _cparams(sem):
    return pltpu.CompilerParams(dimension_semantics=sem, vmem_limit_bytes=VMEM_LIMIT)


def _mod_kernel(c_ref, w_ref, b_ref, o_ref):
    c = c_ref[...]
    s = c * jax.nn.sigmoid(c)
    o_ref[...] = jnp.dot(s.astype(BF16), w_ref[...].astype(BF16),
                         preferred_element_type=F32) + b_ref[...]


def _mod_call(c, w_ada, b_ada):
    B = c.shape[0]
    n = w_ada.shape[1]
    tn = 1536
    return pl.pallas_call(
        _mod_kernel,
        grid=(n // tn,),
        in_specs=[pl.BlockSpec((B, D_MODEL), lambda j: (0, 0)),
                  pl.BlockSpec((D_MODEL, tn), lambda j: (0, j)),
                  pl.BlockSpec((1, tn), lambda j: (0, j))],
        out_specs=pl.BlockSpec((B, tn), lambda j: (0, j)),
        out_shape=jax.ShapeDtypeStruct((B, n), F32),
        compiler_params=_cparams(("arbitrary",)),
        name="mod",
    )(c, w_ada, b_ada.reshape(1, n))


def _rms_scale(xs, width):
    ms = jnp.sum(xs * xs, axis=-1, keepdims=True) * (1.0 / width)
    return xs * lax.rsqrt(ms + EPS)


def _inproj_kernel(x_ref, mod_ref, g1_ref, w_ref, cos_ref, sin_ref, qg_ref, kg_ref, ikg_ref,
                   p_ref, vt_ref, h_ref, *, tm):
    j = pl.program_id(1)

    @pl.when(j == 0)
    def _():
        x = x_ref[...]
        ms = jnp.mean(x * x, axis=-1, keepdims=True)
        y = (x * lax.rsqrt(ms + EPS)) * g1_ref[...]
        h = y * (1.0 + mod_ref[0, 1:2, :]) + mod_ref[0, 0:1, :]
        h_ref[...] = h.astype(BF16)

    half = P_TILE // 2

    def acc_half(n):
        return jnp.dot(h_ref[...], w_ref[:, n * half:(n + 1) * half], preferred_element_type=F32)

    def rotate(scale):
        cos = cos_ref[...]
        sin = sin_ref[...]
        for n in range(2):
            acc = acc_half(n)
            for hh in range(half // LANES):
                xs = acc[:, hh * LANES:(hh + 1) * LANES]
                r = xs * cos + pltpu.roll(xs, LANES // 2, 1) * sin
                if scale != 1.0:
                    r = r * scale
                c0 = n * half + hh * LANES
                p_ref[:, c0:c0 + LANES] = r.astype(BF16)

    @pl.when(j == 0)
    def _():
        rotate(1.0)

    @pl.when(j == 1)
    def _():
        rotate(HEAD_DIM ** -0.5)

    @pl.when(j == 2)
    def _():
        for n in range(2):
            p_ref[:, n * half:(n + 1) * half] = acc_half(n).astype(BF16)

    @pl.when(j == 3)
    def _():
        for n in range(2):
            acc = acc_half(n)
            p_ref[:, n * half:(n + 1) * half] = (acc * jax.nn.sigmoid(acc)).astype(BF16)

    @pl.when(j == 4)
    def _():
        g = qg_ref[...] * (HEAD_DIM ** -0.5 * LOG2E)
        for n in range(2):
            acc = acc_half(n)
            for hh in range(half // LANES):
                xs = acc[:, hh * LANES:(hh + 1) * LANES]
                c0 = n * half + hh * LANES
                p_ref[:, c0:c0 + LANES] = (_rms_scale(xs, HEAD_DIM) * g).astype(BF16)

    @pl.when(j == 5)
    def _():
        for n in range(2):
            p_ref[:, n * half:(n + 1) * half] = (acc_half(n) * (IDX_DIM ** -0.5)).astype(BF16)

    @pl.when(j == 6)
    def _():
        kg = kg_ref[...]
        acc = acc_half(0)
        for g in range(DSA_KV_HEADS):
            xs = acc[:, g * LANES:(g + 1) * LANES]
            p_ref[:, g * LANES:(g + 1) * LANES] = (_rms_scale(xs, HEAD_DIM) * kg).astype(BF16)
        dv = acc[:, 256:512]
        p_ref[:, 256:512] = dv.astype(BF16)
        for s in range(tm // KEY_CHUNK):
            blk = dv[s * KEY_CHUNK:(s + 1) * KEY_CHUNK, :]
            vt_ref[0, s] = blk.T.astype(BF16)
        acc = jnp.dot(h_ref[...], w_ref[:, half:half + 2 * LANES], preferred_element_type=F32)
        ik = acc[:, 0:128]
        ika = _rms_scale(ik, IDX_DIM) * ikg_ref[...]
        p_ref[:, 512:640] = ika.astype(BF16)
        p_ref[:, 640:768] = pltpu.roll(ika, LANES // 2, 1).astype(BF16)
        p_ref[:, 768:896] = (acc[:, 128:256] * (IDX_HEADS ** -0.5)).astype(BF16)
        p_ref[:, 896:1024] = jnp.zeros((tm, LANES), BF16)


def _inproj_call(x2, mod3, g1, w_p, cos_t, sin_t, qg, kg, ikg, *, B, L, tm):
    T = B * L
    tpb = L // tm
    kern = functools.partial(_inproj_kernel, tm=tm)
    return pl.pallas_call(
        kern,
        grid=(T // tm, P_NTILES),
        in_specs=[
            pl.BlockSpec((tm, D_MODEL), lambda i, j: (i, 0)),
            pl.BlockSpec((1, N_MOD, D_MODEL), lambda i, j: (i // tpb, 0, 0)),
            pl.BlockSpec((1, D_MODEL), lambda i, j: (0, 0)),
            pl.BlockSpec((D_MODEL, P_TILE), lambda i, j: (0, j)),
            pl.BlockSpec((tm, LANES), lambda i, j: (i % tpb, 0)),
            pl.BlockSpec((tm, LANES), lambda i, j: (i % tpb, 0)),
            pl.BlockSpec((1, LANES), lambda i, j: (0, 0)),
            pl.BlockSpec((1, LANES), lambda i, j: (0, 0)),
            pl.BlockSpec((1, LANES), lambda i, j: (0, 0)),
        ],
        out_specs=[
            pl.BlockSpec((tm, P_TILE), lambda i, j: (i, j)),
            pl.BlockSpec((1, tm // KEY_CHUNK, KEY_CHUNK, KEY_CHUNK),
                         lambda i, j: (i // tpb, i % tpb, 0, 0)),
        ],
        out_shape=[jax.ShapeDtypeStruct((T, P_WIDTH), BF16),
                   jax.ShapeDtypeStruct((B, L // KEY_CHUNK, KEY_CHUNK, KEY_CHUNK), BF16)],
        scratch_shapes=[pltpu.VMEM((tm, D_MODEL), BF16)],
        compiler_params=_cparams(("arbitrary", "arbitrary")),
        name="inproj",
    )(x2, mod3, g1, w_p, cos_t, sin_t, qg, kg, ikg)


def _ret_consts():
    h = np.arange(RET_HEADS, dtype=np.float32)
    log_gamma = np.log(np.float32(1.0) - np.exp2(np.float32(-5.0) - h)).astype(np.float32)
    return log_gamma


def _retention_kernel(q_ref, k_ref, v_ref, g_ref, dm_ref, xi_ref, zeta_ref, beta_ref,
                      o_ref, r_ref, *, rt, decay):
    @pl.when(pl.program_id(1) == 0)
    def _():
        r_ref[...] = jnp.zeros(r_ref.shape, F32)

    nt = (((1,), (1,)), ((), ()))

    def chunk(c, carry):
        r0 = pl.multiple_of(c * RET_BLOCK, RET_BLOCK)
        for hh in range(RET_HEADS):
            cs = slice(hh * LANES, (hh + 1) * LANES)
            q = q_ref[pl.ds(r0, RET_BLOCK), cs]
            k = k_ref[pl.ds(r0, RET_BLOCK), cs]
            v = v_ref[pl.ds(r0, RET_BLOCK), cs]
            inner = lax.dot_general(q, k, nt, preferred_element_type=F32) * dm_ref[hh]
            state = r_ref[hh]
            o = (jnp.dot(inner.astype(BF16), v, preferred_element_type=F32)
                 + jnp.dot(q, state.astype(BF16), preferred_element_type=F32) * xi_ref[hh])
            kz = k.astype(F32) * zeta_ref[hh]
            r_ref[hh] = state * decay[hh] + jnp.dot(kz.T.astype(BF16), v,
                                                    preferred_element_type=F32)
            mu = jnp.mean(o, axis=-1, keepdims=True)
            d = o - mu
            var = jnp.mean(d * d, axis=-1, keepdims=True)
            on = d * lax.rsqrt(var + EPS)
            y = g_ref[pl.ds(r0, RET_BLOCK), cs].astype(F32) * (on * beta_ref[:, cs])
            o_ref[pl.ds(r0, RET_BLOCK), cs] = y.astype(BF16)
        return carry

    lax.fori_loop(0, rt // RET_BLOCK, chunk, 0)


def _retention_call(P, dmask, xi_b, zeta_b, beta, decay, *, B, L, rt):
    T = B * L
    nb = L // rt
    kern = functools.partial(_retention_kernel, rt=rt, decay=decay)
    blk = lambda col: pl.BlockSpec((rt, P_TILE), lambda b, l, col=col: (b * nb + l, col))
    cst = lambda w: pl.BlockSpec((RET_HEADS, RET_BLOCK, w), lambda b, l: (0, 0, 0))
    return pl.pallas_call(
        kern,
        grid=(B, nb),
        in_specs=[blk(0), blk(1), blk(2), blk(3), cst(RET_BLOCK), cst(LANES), cst(LANES),
                  pl.BlockSpec((1, RET_W), lambda b, l: (0, 0))],
        out_specs=pl.BlockSpec((rt, RET_W), lambda b, l: (b * nb + l, 0)),
        out_shape=jax.ShapeDtypeStruct((T, RET_W), BF16),
        scratch_shapes=[pltpu.VMEM((RET_HEADS, HEAD_DIM, HEAD_DIM), F32)],
        compiler_params=_cparams(("arbitrary", "arbitrary")),
        name="retention",
    )(P, P, P, P, dmask, xi_b, zeta_b, beta)


def _sortable_key(x):
    b = pltpu.bitcast(x, I32)
    return b ^ ((b >> 31) & 0x7FFFFFFF)


def _tree_sum(parts):
    parts = list(parts)
    while len(parts) > 1:
        nxt = [parts[k] + parts[k + 1] for k in range(0, len(parts) - 1, 2)]
        if len(parts) % 2:
            nxt.append(parts[-1])
        parts = nxt
    return parts[0]


def _bit_transpose32(a):
    a = list(a)
    for k, m in ((16, 0x0000FFFF), (8, 0x00FF00FF), (4, 0x0F0F0F0F), (2, 0x33333333),
                 (1, 0x55555555)):
        for i in range(32):
            if i & k == 0:
                t = ((a[i] >> k) ^ a[i + k]) & jnp.uint32(m)
                a[i + k] = a[i + k] ^ t
                a[i] = a[i] ^ (t << k)
    return a


def _dsa_kernel(dq_ref, iq_ref, iw_ref, iqn_ref, iwn_ref, dk_ref, vt_ref, ika_ref, ikb_ref,
                beta_ref, o_ref,
                key0_ref, key1_ref, plane0_ref, plane1_ref, eq_ref, sta_ref, stb_ref, lt_ref,
                qt_ref, acc_ref, m_ref, l_ref, *, topk, nblk, nq):
    j = pl.program_id(1)

    def npairs(i):
        return ((i + 2) // 2 + 1) // 2

    @pl.when(jnp.logical_and(pl.program_id(0) == 0, j == 0))
    def _():
        plane0_ref[...] = jnp.zeros(plane0_ref.shape, U32)
        plane1_ref[...] = jnp.zeros(plane1_ref.shape, U32)

    krow = lax.broadcasted_iota(I32, (KEY_CHUNK, Q_BLOCK), 0)
    qlane = lax.broadcasted_iota(I32, (KEY_CHUNK, Q_BLOCK), 1)

    def prep_scores(iq_blk, iw_blk):
        iq = iq_blk.astype(F32)
        for p in range(IDX_HEADS // 2):
            lt_ref[:, p * LANES:(p + 1) * LANES] = iq[:, p * LANES:(p + 1) * LANES].T.astype(BF16)
        iwt = iw_blk.astype(F32).T
        w_even = jnp.concatenate([iwt[2 * p:2 * p + 1, :] for p in range(IDX_HEADS // 2)], axis=1)
        w_odd = jnp.concatenate([iwt[2 * p + 1:2 * p + 2, :] for p in range(IDX_HEADS // 2)],
                                axis=1)
        return w_even, w_odd

    def score_chunk(c, q0, w_even, w_odd, key_ref, plane_ref, masked=True):
        k0 = pl.multiple_of(c * KEY_CHUNK, KEY_CHUNK)
        lt = lt_ref[...]
        se = jnp.dot(ika_ref[pl.ds(k0, KEY_CHUNK), :], lt, preferred_element_type=F32)
        so = jnp.dot(ikb_ref[pl.ds(k0, KEY_CHUNK), :], lt, preferred_element_type=F32)
        te = jnp.maximum(se, 0.0) * w_even
        to = jnp.maximum(so, 0.0) * w_odd
        sc = jnp.zeros((KEY_CHUNK, Q_BLOCK), F32)
        for p in range(IDX_HEADS // 2):
            sc = sc + te[:, p * LANES:(p + 1) * LANES]
            sc = sc + to[:, p * LANES:(p + 1) * LANES]
        key = _sortable_key(sc)
        if masked:
            key = jnp.where(krow + k0 <= qlane + q0, key, INT_MIN)
        key_ref[pl.ds(k0, KEY_CHUNK), :] = key
        u = pltpu.bitcast(key ^ INT_MIN, U32)
        planes = _bit_transpose32([u[8 * r:8 * r + 8, :] for r in range(32)])
        for b in range(32):
            plane_ref[b, c] = planes[b]

    def search(key_ref, plane_ref, npair):
        blk_id = lax.broadcasted_iota(I32, (nblk, 8, Q_BLOCK), 0)
        eq_ref[...] = jnp.where(blk_id < 2 * npair, jnp.uint32(0xFFFFFFFF), jnp.uint32(0))

        def make_pass(nb):
            def search_pass(k, carry):
                t_u, cnt_gt = carry
                b = 31 - k
                eq = eq_ref[0:nb]
                t = eq & plane_ref[b, 0:nb]
                pc = lax.population_count(t)
                tot = jnp.sum(_tree_sum([pc[r] for r in range(nb)]).astype(I32), axis=0,
                              keepdims=True)
                tot = jnp.broadcast_to(tot, (8, Q_BLOCK))
                take = cnt_gt + tot >= topk
                keep_eq = jnp.where(take, jnp.uint32(0), jnp.uint32(0xFFFFFFFF))
                eq_ref[0:nb] = (eq & keep_eq[None]) ^ t
                bit = jnp.left_shift(jnp.int32(1), b)
                return (jnp.where(take, t_u | bit, t_u), jnp.where(take, cnt_gt, cnt_gt + tot))
            return search_pass

        zero = (jnp.zeros((8, Q_BLOCK), I32), jnp.zeros((8, Q_BLOCK), I32))
        t_u, cnt_gt = lax.cond(
            2 * npair <= nblk // 2,
            lambda: lax.fori_loop(0, 32, make_pass(nblk // 2), zero),
            lambda: lax.fori_loop(0, 32, make_pass(nblk), zero))
        thr = jnp.maximum(t_u[0:1, :] ^ INT_MIN, INT_MIN + 1)

        pc_eq = lax.population_count(eq_ref[...])
        n_eq = jnp.sum(_tree_sum([pc_eq[r] for r in range(nblk)]).astype(I32), axis=0,
                       keepdims=True)
        excess = cnt_gt[0:1, :] + n_eq - topk

        @pl.when(jnp.max(excess) > 0)
        def _():
            keep = jnp.where(excess > 0, topk - cnt_gt[0:1, :], jnp.int32(2 ** 30))

            def count_tied_below(q_cut):
                def step(c, cnt):
                    k0 = pl.multiple_of(c * KEY_CHUNK, KEY_CHUNK)
                    kk = key_ref[pl.ds(k0, KEY_CHUNK), :]
                    hit = jnp.where(kk == thr, jnp.where(krow + k0 < q_cut, 1, 0), 0)
                    return cnt + jnp.sum(hit, axis=0, keepdims=True)
                return lax.fori_loop(0, 2 * npair, step, jnp.zeros((1, Q_BLOCK), I32))

            def pos_pass(k, q_max):
                cand = q_max | jnp.left_shift(jnp.int32(1), POS_BITS - 1 - k)
                return jnp.where(count_tied_below(cand) < keep, cand, q_max)

            q_max = lax.fori_loop(0, POS_BITS, pos_pass, jnp.zeros((1, Q_BLOCK), I32))

            def drop_step(c, carry):
                k0 = pl.multiple_of(c * KEY_CHUNK, KEY_CHUNK)
                kk = key_ref[pl.ds(k0, KEY_CHUNK), :]
                drop = jnp.where(kk == thr, jnp.where(krow + k0 > q_max, 1, 0), 0)
                key_ref[pl.ds(k0, KEY_CHUNK), :] = jnp.where(drop > 0, INT_MIN, kk)
                return carry

            lax.fori_loop(0, 2 * npair, drop_step, 0)

        return thr

    ones_rows = jnp.ones((16, KEY_CHUNK), BF16)

    def qk_dots(c, st_ref):
        k0 = pl.multiple_of(c * KEY_CHUNK, KEY_CHUNK)
        for g in range(DSA_KV_HEADS):
            kch = dk_ref[pl.ds(k0, KEY_CHUNK), g * LANES:(g + 1) * LANES]
            qt = qt_ref[:, g * DSA_REP * LANES:(g + 1) * DSA_REP * LANES]
            st_ref[g] = jnp.dot(kch, qt, preferred_element_type=F32)

    def softmax_pv(c, st_ref, key_ref, thr):
        k0 = pl.multiple_of(c * KEY_CHUNK, KEY_CHUNK)
        sel = key_ref[pl.ds(k0, KEY_CHUNK), :] >= thr
        for g in range(DSA_KV_HEADS):
            st = st_ref[g]
            st = jnp.concatenate(
                [jnp.where(sel, st[:, r * LANES:(r + 1) * LANES], NEG_BIG) for r in range(DSA_REP)],
                axis=1)
            m_old = m_ref[g:g + 1, :]
            m_new = jnp.maximum(m_old, jnp.max(st, axis=0, keepdims=True))
            alpha = jnp.exp2(m_old - m_new)
            p = jnp.exp2(st - m_new)
            m_ref[g:g + 1, :] = m_new
            vt = jnp.concatenate([vt_ref[0, c, g * LANES:(g + 1) * LANES, :], ones_rows], axis=0)
            pv = jnp.dot(vt, p.astype(BF16), preferred_element_type=F32)
            acc_ref[g] = acc_ref[g] * alpha + pv[:HEAD_DIM]
            l_ref[g:g + 1, :] = alpha * l_ref[g:g + 1, :] + pv[HEAD_DIM:HEAD_DIM + 1]

    def half_step(i_att, key_a, plane_a, row0, iq_blk, iw_blk, q0_sc, ps, key_s, plane_s):
        pa = npairs(i_att)
        thr = search(key_a, plane_a, pa)

        dq = dq_ref[row0:row0 + Q_BLOCK, :].astype(F32)
        for hh in range(DSA_HEADS):
            qt_ref[:, hh * LANES:(hh + 1) * LANES] = (
                dq[:, hh * LANES:(hh + 1) * LANES].T.astype(BF16))
        w_even, w_odd = prep_scores(iq_blk, iw_blk)
        acc_ref[...] = jnp.zeros(acc_ref.shape, F32)
        m_ref[...] = jnp.full(m_ref.shape, NEG_BIG, F32)
        l_ref[...] = jnp.zeros(l_ref.shape, F32)

        def score_pair(p, masked=True):
            score_chunk(2 * p, q0_sc, w_even, w_odd, key_s, plane_s, masked)
            score_chunk(2 * p + 1, q0_sc, w_even, w_odd, key_s, plane_s, masked)

        def fused_body(p, carry):
            c = 2 * p
            qk_dots(c + 1, stb_ref)
            softmax_pv(c, sta_ref, key_a, thr)
            qk_dots(c + 2, sta_ref)
            softmax_pv(c + 1, stb_ref, key_a, thr)
            score_pair(p, masked=False)
            return carry

        qk_dots(0, sta_ref)
        lax.fori_loop(0, pa - 1, fused_body, 0)
        c_last = 2 * (pa - 1)
        qk_dots(c_last + 1, stb_ref)
        softmax_pv(c_last, sta_ref, key_a, thr)
        softmax_pv(c_last + 1, stb_ref, key_a, thr)
        score_pair(pa - 1)

        @pl.when(ps > pa)
        def _():
            score_pair(pa)

        beta = beta_ref[...]
        for g in range(DSA_KV_HEADS):
            ot = acc_ref[g] / l_ref[g:g + 1, :]
            for r in range(DSA_REP):
                hh = g * DSA_REP + r
                blk = ot[:, r * LANES:(r + 1) * LANES].T
                o_ref[row0:row0 + Q_BLOCK, hh * LANES:(hh + 1) * LANES] = (
                    blk * beta[:, hh * LANES:(hh + 1) * LANES]).astype(BF16)

    i0 = 2 * j
    i1 = i0 + 1
    i2 = i0 + 2

    @pl.when(j == 0)
    def _():
        w_even, w_odd = prep_scores(iq_ref[0:Q_BLOCK, :], iw_ref[0:Q_BLOCK, :])
        score_chunk(0, 0, w_even, w_odd, key0_ref, plane0_ref)
        score_chunk(1, 0, w_even, w_odd, key0_ref, plane0_ref)

    half_step(i0, key0_ref, plane0_ref, 0,
              iq_ref[Q_BLOCK:2 * Q_BLOCK, :], iw_ref[Q_BLOCK:2 * Q_BLOCK, :], i1 * Q_BLOCK,
              npairs(i1), key1_ref, plane1_ref)
    has_next = i2 < nq
    half_step(i1, key1_ref, plane1_ref, Q_BLOCK,
              iqn_ref[...], iwn_ref[...], jnp.minimum(i2, nq - 1) * Q_BLOCK,
              jnp.where(has_next, npairs(i2), npairs(i1)), key0_ref, plane0_ref)


def _dsa_call(P, vT, beta, *, B, L, topk):
    T = B * L
    nq = L // Q_BLOCK
    nq2 = nq // 2
    nblk = L // KEY_CHUNK
    kern = functools.partial(_dsa_kernel, topk=topk, nblk=nblk, nq=nq)
    two = 2 * Q_BLOCK
    qrow2 = lambda col, width: pl.BlockSpec((two, width),
                                            lambda b, j, col=col: (b * nq2 + j, col))
    nxt = lambda col, width: pl.BlockSpec(
        (Q_BLOCK, width), lambda b, j, col=col: (b * nq + jnp.minimum(2 * j + 2, nq - 1), col))
    tail128 = TAIL_COL // LANES
    plane = pltpu.VMEM((32, nblk, 8, Q_BLOCK), U32)
    return pl.pallas_call(
        kern,
        grid=(B, nq2),
        in_specs=[
            qrow2(4, P_TILE),
            qrow2(5, P_TILE),
            qrow2(tail128 + 6, LANES),
            nxt(5, P_TILE),
            nxt(tail128 + 6, LANES),
            pl.BlockSpec((L, DSA_KV_W), lambda b, j: (b, TAIL_COL // DSA_KV_W)),
            pl.BlockSpec((1, L // KEY_CHUNK, KEY_CHUNK, KEY_CHUNK), lambda b, j: (b, 0, 0, 0)),
            pl.BlockSpec((L, LANES), lambda b, j: (b, tail128 + 4)),
            pl.BlockSpec((L, LANES), lambda b, j: (b, tail128 + 5)),
            pl.BlockSpec((1, DSA_W), lambda b, j: (0, 0)),
        ],
        out_specs=pl.BlockSpec((two, DSA_W), lambda b, j: (b * nq2 + j, 0)),
        out_shape=jax.ShapeDtypeStruct((T, DSA_W), BF16),
        scratch_shapes=[
            pltpu.VMEM((L, Q_BLOCK), I32), pltpu.VMEM((L, Q_BLOCK), I32),
            plane, plane,
            pltpu.VMEM((nblk, 8, Q_BLOCK), U32),
            pltpu.VMEM((DSA_KV_HEADS, KEY_CHUNK, DSA_REP * Q_BLOCK), F32),
            pltpu.VMEM((DSA_KV_HEADS, KEY_CHUNK, DSA_REP * Q_BLOCK), F32),
            pltpu.VMEM((LANES, P_TILE), BF16),
            pltpu.VMEM((HEAD_DIM, DSA_W), BF16),
            pltpu.VMEM((DSA_KV_HEADS, HEAD_DIM, DSA_REP * Q_BLOCK), F32),
            pltpu.VMEM((DSA_KV_HEADS, DSA_REP * Q_BLOCK), F32),
            pltpu.VMEM((DSA_KV_HEADS, DSA_REP * Q_BLOCK), F32),
        ],
        compiler_params=_cparams(("arbitrary", "arbitrary")),
        name="dsa",
    )(P, P, P, P, P, P, vT, P, P, beta)


def _outproj_kernel(yr_ref, yd_ref, x_ref, mod_ref, g2_ref, w_ref, o_ref, h_ref, *, tm):
    half = tm // 2
    for r in range(2):
        rows = slice(r * half, (r + 1) * half)
        mix = (jnp.dot(yr_ref[rows, :], w_ref[0:RET_W, :], preferred_element_type=F32)
               + jnp.dot(yd_ref[rows, :], w_ref[RET_W:RET_W + DSA_W, :],
                         preferred_element_type=F32))
        x1 = x_ref[rows, :] + mod_ref[0, 2:3, :] * mix
        o_ref[rows, :] = x1
        ms = jnp.mean(x1 * x1, axis=-1, keepdims=True)
        y = (x1 * lax.rsqrt(ms + EPS)) * g2_ref[...]
        h_ref[rows, :] = (y * (1.0 + mod_ref[0, 4:5, :]) + mod_ref[0, 3:4, :]).astype(BF16)


def _outproj_call(y_ret, y_dsa, x2, mod3, g2, w_out, *, B, L, tm):
    T = B * L
    tpb = L // tm
    row = lambda w: pl.BlockSpec((tm, w), lambda i: (i, 0))
    return pl.pallas_call(
        functools.partial(_outproj_kernel, tm=tm),
        grid=(T // tm,),
        in_specs=[
            row(RET_W), row(DSA_W), row(D_MODEL),
            pl.BlockSpec((1, N_MOD, D_MODEL), lambda i: (i // tpb, 0, 0)),
            pl.BlockSpec((1, D_MODEL), lambda i: (0, 0)),
            pl.BlockSpec((RET_W + DSA_W, D_MODEL), lambda i: (0, 0)),
        ],
        out_specs=[row(D_MODEL), row(D_MODEL)],
        out_shape=[jax.ShapeDtypeStruct((T, D_MODEL), F32),
                   jax.ShapeDtypeStruct((T, D_MODEL), BF16)],
        compiler_params=_cparams(("arbitrary",)),
        name="outproj",
    )(y_ret, y_dsa, x2, mod3, g2, w_out)


def _mlp_kernel(h_ref, xs_ref, gate_ref, w1_ref, w2_ref, o_ref, a_ref, *, tm):
    s = pl.program_id(1)

    @pl.when(s < MLP_NA)
    def _():
        hid = jnp.dot(h_ref[...], w1_ref[...], preferred_element_type=F32)
        a = jnp.maximum(hid, 0.0)
        a_ref[s] = (a * a).astype(BF16)

    @pl.when(s >= MLP_NA)
    def _():
        half = tm // 2
        for r in range(2):
            rows = slice(r * half, (r + 1) * half)
            y = None
            for f in range(MLP_NA):
                d = jnp.dot(a_ref[f, rows, :], w2_ref[f * MLP_TF:(f + 1) * MLP_TF, :],
                            preferred_element_type=F32)
                y = d if y is None else y + d
            o_ref[rows, :] = xs_ref[rows, :] + gate_ref[0] * y


def _mlp_call(h2, x1, mod3, w1, w2, *, B, L, tm):
    T = B * L
    tpb = L // tm
    gate2 = mod3[:, 5:6, :]
    kern = functools.partial(_mlp_kernel, tm=tm)
    ncol = lambda s: jnp.maximum(s - MLP_NA, 0)
    return pl.pallas_call(
        kern,
        grid=(T // tm, MLP_NA + MLP_NB),
        in_specs=[
            pl.BlockSpec((tm, D_MODEL), lambda i, s: (i, 0)),
            pl.BlockSpec((tm, MLP_TN), lambda i, s: (i, ncol(s))),
            pl.BlockSpec((1, 1, MLP_TN), lambda i, s: (i // tpb, 0, ncol(s))),
            pl.BlockSpec((D_MODEL, MLP_TF), lambda i, s: (0, jnp.minimum(s, MLP_NA - 1))),
            pl.BlockSpec((D_FF, MLP_TN), lambda i, s: (0, ncol(s))),
        ],
        out_specs=pl.BlockSpec((tm, MLP_TN), lambda i, s: (i, ncol(s))),
        out_shape=jax.ShapeDtypeStruct((T, D_MODEL), F32),
        scratch_shapes=[pltpu.VMEM((MLP_NA, tm, MLP_TF), BF16)],
        compiler_params=_cparams(("arbitrary", "arbitrary")),
        name="mlp",
    )(h2, x1, gate2, w1, w2)


def _deinterleave_heads(w):
    k = w.shape[0]
    return w.reshape(k, RET_HEADS, HEAD_DIM // 2, 2).transpose(0, 1, 3, 2).reshape(k, RET_W)


def _prep_w_in(w):
    w = w.astype(BF16)
    o = np.cumsum([0, RET_W, RET_W, RET_W, RET_W, DSA_W, DSA_KV_W, DSA_KV_W,
                   IDX_HEADS * IDX_DIM, IDX_DIM, IDX_HEADS])
    part = lambda n: w[:, o[n]:o[n + 1]]
    z = lambda n: jnp.zeros((w.shape[0], n), w.dtype)
    cols = [_deinterleave_heads(part(0)), _deinterleave_heads(part(1)), part(2), part(3),
            part(4), part(7),
            part(5), part(6), part(8), z(LANES - IDX_DIM),
            part(9), z(LANES - IDX_HEADS), z(2 * LANES)]
    return jnp.concatenate(cols, axis=1)


def _rot_tables(L):
    pos = np.arange(L, dtype=np.float32)
    angle = (np.float32(1.0) / np.float32(10000.0) ** np.linspace(0.0, 1.0, HEAD_DIM // 2,
                                                                  dtype=np.float32)).astype(np.float32)
    theta = (pos[:, None] * angle[None, :]).astype(np.float32)
    s, c = np.sin(theta).astype(np.float32), np.cos(theta).astype(np.float32)
    return np.concatenate([c, c], axis=1), np.concatenate([-s, s], axis=1)


def _ret_tables():
    log_gamma = _ret_consts()
    C = RET_BLOCK
    pos = np.arange(C, dtype=np.float32)
    diff = pos[:, None] - pos[None, :]
    dmask = np.where(diff[None] >= 0,
                     np.exp(log_gamma[:, None, None] * np.maximum(diff, 0.0)[None]), 0.0)
    xi = np.exp(log_gamma[:, None] * (pos[None, :] + np.float32(1.0)))
    zeta = np.exp(log_gamma[:, None] * (np.float32(C - 1.0) - pos[None, :]))
    bc = lambda a: np.ascontiguousarray(
        np.broadcast_to(a.astype(np.float32)[:, :, None], (RET_HEADS, C, LANES)))
    return dmask.astype(np.float32), bc(xi), bc(zeta)


def _pad_lanes(v, n):
    return jnp.concatenate([v, jnp.zeros((n - v.shape[0],), v.dtype)]).reshape(1, n)


def _layer(x, c, w_ada, b_ada, norm1_g, norm2_g, w_in, ret_beta, q_norm_g, k_norm_g,
           idx_k_norm_g, dsa_beta, w_out, w_mlp1, w_mlp2):
    B, L, _ = x.shape
    assert L % (2 * KEY_CHUNK) == 0 and L <= 2 ** (POS_BITS - 1)
    topk = min(TOPK_MAX, L // 4)
    tm = min(1024, L)
    x2 = x.reshape(B * L, D_MODEL)

    mod3 = _mod_call(c, w_ada, b_ada).reshape(B, N_MOD, D_MODEL)

    cos_t, sin_t = _rot_tables(L)
    P, vT = _inproj_call(x2, mod3, norm1_g.reshape(1, -1), _prep_w_in(w_in), cos_t, sin_t,
                         q_norm_g.reshape(1, -1), k_norm_g.reshape(1, -1),
                         _pad_lanes(idx_k_norm_g, LANES), B=B, L=L, tm=tm)

    dmask, xi_b, zeta_b = _ret_tables()
    decay = tuple(float(v) for v in np.exp(_ret_consts() * np.float32(RET_BLOCK)))
    y_ret = _retention_call(P, dmask, xi_b, zeta_b, ret_beta.reshape(1, -1), decay,
                            B=B, L=L, rt=tm)
    y_dsa = _dsa_call(P, vT, dsa_beta.reshape(1, -1), B=B, L=L, topk=topk)

    x1, h2 = _outproj_call(y_ret, y_dsa, x2, mod3, norm2_g.reshape(1, -1), w_out.astype(BF16),
                           B=B, L=L, tm=min(512, L))
    out = _mlp_call(h2, x1, mod3, w_mlp1.astype(BF16), w_mlp2.astype(BF16), B=B, L=L, tm=tm)
    return out.reshape(B, L, D_MODEL)


def kernel(x, c, w_ada, b_ada, norm1_g, norm2_g, w_in, ret_beta, q_norm_g, k_norm_g,
           idx_k_norm_g, dsa_beta, w_out, w_mlp1, w_mlp2):
    for l in range(w_ada.shape[0]):
        x = _layer(x, c, w_ada[l], b_ada[l], norm1_g[l], norm2_g[l], w_in[l], ret_beta[l],
                   q_norm_g[l], k_norm_g[l], idx_k_norm_g[l], dsa_beta[l], w_out[l],
                   w_mlp1[l], w_mlp2[l])
    return x
```

```python
import functools

import numpy as np
import jax
import jax.numpy as jnp
from jax import lax
from jax.experimental import pallas as pl
from jax.experimental.pallas import tpu as pltpu

F32 = jnp.float32
BF16 = jnp.bfloat16
I32 = jnp.int32
U32 = jnp.uint32

D_MODEL = 2048
HEAD_DIM = 128
RET_HEADS = 8
RET_W = 1024
RET_CHUNK = 128
RET_BLOCK = 256
DSA_HEADS = 8
DSA_KV_HEADS = 2
DSA_REP = DSA_HEADS // DSA_KV_HEADS
DSA_W = 1024
DSA_KV_W = 256
IDX_HEADS = 16
IDX_DIM = 64
TOPK_MAX = 256
Q_BLOCK = 128
D_FF = 4 * D_MODEL
N_MOD = 6
EPS = 1e-6

LANES = 128
KEY_CHUNK = 256
MLP_TF = 1024
MLP_TN = 256
MLP_NA = D_FF // MLP_TF
MLP_NB = D_MODEL // MLP_TN
VMEM_LIMIT = 56 * 1024 * 1024

P_TILE = 1024
P_NTILES = 7
P_WIDTH = P_TILE * P_NTILES
TAIL_COL = 6 * P_TILE

POS_BITS = 14
INT_MIN = -(2 ** 31)
NEG_BIG = -1e30
LOG2E = 1.4426950408889634


def _cparams(sem):
    return pltpu.CompilerParams(dimension_semantics=sem, vmem_limit_bytes=VMEM_LIMIT)


def _mod_kernel(c_ref, w_ref, b_ref, o_ref):
    c = c_ref[...]
    s = c * jax.nn.sigmoid(c)
    o_ref[...] = jnp.dot(s.astype(BF16), w_ref[...].astype(BF16),
                         preferred_element_type=F32) + b_ref[...]


def _mod_call(c, w_ada, b_ada):
    B = c.shape[0]
    n = w_ada.shape[1]
    tn = 1536
    return pl.pallas_call(
        _mod_kernel,
        grid=(n // tn,),
        in_specs=[pl.BlockSpec((B, D_MODEL), lambda j: (0, 0)),
                  pl.BlockSpec((D_MODEL, tn), lambda j: (0, j)),
                  pl.BlockSpec((1, tn), lambda j: (0, j))],
        out_specs=pl.BlockSpec((B, tn), lambda j: (0, j)),
        out_shape=jax.ShapeDtypeStruct((B, n), F32),
        compiler_params=_cparams(("arbitrary",)),
        name="mod",
    )(c, w_ada, b_ada.reshape(1, n))


def _rms_scale(xs, width):
    ms = jnp.sum(xs * xs, axis=-1, keepdims=True) * (1.0 / width)
    return xs * lax.rsqrt(ms + EPS)


def _inproj_kernel(x_ref, mod_ref, g1_ref, w_ref, cos_ref, sin_ref, qg_ref, kg_ref, ikg_ref,
                   p_ref, vt_ref, h_ref, *, tm):
    j = pl.program_id(1)

    @pl.when(j == 0)
    def _():
        x = x_ref[...]
        ms = jnp.mean(x * x, axis=-1, keepdims=True)
        y = (x * lax.rsqrt(ms + EPS)) * g1_ref[...]
        h = y * (1.0 + mod_ref[0, 1:2, :]) + mod_ref[0, 0:1, :]
        h_ref[...] = h.astype(BF16)

    half = P_TILE // 2

    def acc_half(n):
        return jnp.dot(h_ref[...], w_ref[:, n * half:(n + 1) * half], preferred_element_type=F32)

    def rotate(scale):
        cos = cos_ref[...]
        sin = sin_ref[...]
        for n in range(2):
            acc = acc_half(n)
            for hh in range(half // LANES):
                xs = acc[:, hh * LANES:(hh + 1) * LANES]
                r = xs * cos + pltpu.roll(xs, LANES // 2, 1) * sin
                if scale != 1.0:
                    r = r * scale
                c0 = n * half + hh * LANES
                p_ref[:, c0:c0 + LANES] = r.astype(BF16)

    @pl.when(j == 0)
    def _():
        rotate(1.0)

    @pl.when(j == 1)
    def _():
        rotate(HEAD_DIM ** -0.5)

    @pl.when(j == 2)
    def _():
        for n in range(2):
            p_ref[:, n * half:(n + 1) * half] = acc_half(n).astype(BF16)

    @pl.when(j == 3)
    def _():
        for n in range(2):
            acc = acc_half(n)
            p_ref[:, n * half:(n + 1) * half] = (acc * jax.nn.sigmoid(acc)).astype(BF16)

    @pl.when(j == 4)
    def _():
        g = qg_ref[...] * (HEAD_DIM ** -0.5 * LOG2E)
        for n in range(2):
            acc = acc_half(n)
            for hh in range(half // LANES):
                xs = acc[:, hh * LANES:(hh + 1) * LANES]
                c0 = n * half + hh * LANES
                p_ref[:, c0:c0 + LANES] = (_rms_scale(xs, HEAD_DIM) * g).astype(BF16)

    @pl.when(j == 5)
    def _():
        for n in range(2):
            p_ref[:, n * half:(n + 1) * half] = (acc_half(n) * (IDX_DIM ** -0.5)).astype(BF16)

    @pl.when(j == 6)
    def _():
        kg = kg_ref[...]
        acc = acc_half(0)
        for g in range(DSA_KV_HEADS):
            xs = acc[:, g * LANES:(g + 1) * LANES]
            p_ref[:, g * LANES:(g + 1) * LANES] = (_rms_scale(xs, HEAD_DIM) * kg).astype(BF16)
        dv = acc[:, 256:512]
        p_ref[:, 256:512] = dv.astype(BF16)
        for s in range(tm // KEY_CHUNK):
            blk = dv[s * KEY_CHUNK:(s + 1) * KEY_CHUNK, :]
            vt_ref[0, s] = blk.T.astype(BF16)
        acc = acc_half(1)
        ik = acc[:, 0:128]
        ika = _rms_scale(ik, IDX_DIM) * ikg_ref[...]
        p_ref[:, 512:640] = ika.astype(BF16)
        p_ref[:, 640:768] = pltpu.roll(ika, LANES // 2, 1).astype(BF16)
        p_ref[:, 768:896] = (acc[:, 256:384] * (IDX_HEADS ** -0.5)).astype(BF16)
        p_ref[:, 896:1024] = jnp.zeros((tm, LANES), BF16)


def _inproj_call(x2, mod3, g1, w_p, cos_t, sin_t, qg, kg, ikg, *, B, L, tm):
    T = B * L
    tpb = L // tm
    kern = functools.partial(_inproj_kernel, tm=tm)
    return pl.pallas_call(
        kern,
        grid=(T // tm, P_NTILES),
        in_specs=[
            pl.BlockSpec((tm, D_MODEL), lambda i, j: (i, 0)),
            pl.BlockSpec((1, N_MOD, D_MODEL), lambda i, j: (i // tpb, 0, 0)),
            pl.BlockSpec((1, D_MODEL), lambda i, j: (0, 0)),
            pl.BlockSpec((D_MODEL, P_TILE), lambda i, j: (0, j)),
            pl.BlockSpec((tm, LANES), lambda i, j: (i % tpb, 0)),
            pl.BlockSpec((tm, LANES), lambda i, j: (i % tpb, 0)),
            pl.BlockSpec((1, LANES), lambda i, j: (0, 0)),
            pl.BlockSpec((1, LANES), lambda i, j: (0, 0)),
            pl.BlockSpec((1, LANES), lambda i, j: (0, 0)),
        ],
        out_specs=[
            pl.BlockSpec((tm, P_TILE), lambda i, j: (i, j)),
            pl.BlockSpec((1, tm // KEY_CHUNK, KEY_CHUNK, KEY_CHUNK),
                         lambda i, j: (i // tpb, i % tpb, 0, 0)),
        ],
        out_shape=[jax.ShapeDtypeStruct((T, P_WIDTH), BF16),
                   jax.ShapeDtypeStruct((B, L // KEY_CHUNK, KEY_CHUNK, KEY_CHUNK), BF16)],
        scratch_shapes=[pltpu.VMEM((tm, D_MODEL), BF16)],
        compiler_params=_cparams(("arbitrary", "arbitrary")),
        name="inproj",
    )(x2, mod3, g1, w_p, cos_t, sin_t, qg, kg, ikg)


def _ret_consts():
    h = np.arange(RET_HEADS, dtype=np.float32)
    log_gamma = np.log(np.float32(1.0) - np.exp2(np.float32(-5.0) - h)).astype(np.float32)
    return log_gamma


def _retention_kernel(q_ref, k_ref, v_ref, g_ref, dm_ref, xi_ref, zeta_ref, beta_ref,
                      o_ref, r_ref, *, rt, decay):
    @pl.when(pl.program_id(1) == 0)
    def _():
        r_ref[...] = jnp.zeros(r_ref.shape, F32)

    nt = (((1,), (1,)), ((), ()))

    def chunk(c, carry):
        r0 = pl.multiple_of(c * RET_BLOCK, RET_BLOCK)
        for hh in range(RET_HEADS):
            cs = slice(hh * LANES, (hh + 1) * LANES)
            q = q_ref[pl.ds(r0, RET_BLOCK), cs]
            k = k_ref[pl.ds(r0, RET_BLOCK), cs]
            v = v_ref[pl.ds(r0, RET_BLOCK), cs]
            inner = lax.dot_general(q, k, nt, preferred_element_type=F32) * dm_ref[hh]
            state = r_ref[hh]
            o = (jnp.dot(inner.astype(BF16), v, preferred_element_type=F32)
                 + jnp.dot(q, state.astype(BF16), preferred_element_type=F32) * xi_ref[hh])
            kz = k.astype(F32) * zeta_ref[hh]
            r_ref[hh] = state * decay[hh] + jnp.dot(kz.T.astype(BF16), v,
                                                    preferred_element_type=F32)
            mu = jnp.mean(o, axis=-1, keepdims=True)
            d = o - mu
            var = jnp.mean(d * d, axis=-1, keepdims=True)
            on = d * lax.rsqrt(var + EPS)
            y = g_ref[pl.ds(r0, RET_BLOCK), cs].astype(F32) * (on * beta_ref[:, cs])
            o_ref[pl.ds(r0, RET_BLOCK), cs] = y.astype(BF16)
        return carry

    lax.fori_loop(0, rt // RET_BLOCK, chunk, 0)


def _retention_call(P, dmask, xi_b, zeta_b, beta, decay, *, B, L, rt):
    T = B * L
    nb = L // rt
    kern = functools.partial(_retention_kernel, rt=rt, decay=decay)
    blk = lambda col: pl.BlockSpec((rt, P_TILE), lambda b, l, col=col: (b * nb + l, col))
    cst = lambda w: pl.BlockSpec((RET_HEADS, RET_BLOCK, w), lambda b, l: (0, 0, 0))
    return pl.pallas_call(
        kern,
        grid=(B, nb),
        in_specs=[blk(0), blk(1), blk(2), blk(3), cst(RET_BLOCK), cst(LANES), cst(LANES),
                  pl.BlockSpec((1, RET_W), lambda b, l: (0, 0))],
        out_specs=pl.BlockSpec((rt, RET_W), lambda b, l: (b * nb + l, 0)),
        out_shape=jax.ShapeDtypeStruct((T, RET_W), BF16),
        scratch_shapes=[pltpu.VMEM((RET_HEADS, HEAD_DIM, HEAD_DIM), F32)],
        compiler_params=_cparams(("arbitrary", "arbitrary")),
        name="retention",
    )(P, P, P, P, dmask, xi_b, zeta_b, beta)


def _sortable_key(x):
    b = pltpu.bitcast(x, I32)
    return b ^ ((b >> 31) & 0x7FFFFFFF)


def _tree_sum(parts):
    parts = list(parts)
    while len(parts) > 1:
        nxt = [parts[k] + parts[k + 1] for k in range(0, len(parts) - 1, 2)]
        if len(parts) % 2:
            nxt.append(parts[-1])
        parts = nxt
    return parts[0]


def _bit_transpose32(a):
    a = list(a)
    for k, m in ((16, 0x0000FFFF), (8, 0x00FF00FF), (4, 0x0F0F0F0F), (2, 0x33333333),
                 (1, 0x55555555)):
        for i in range(32):
            if i & k == 0:
                t = ((a[i] >> k) ^ a[i + k]) & jnp.uint32(m)
                a[i + k] = a[i + k] ^ t
                a[i] = a[i] ^ (t << k)
    return a


def _dsa_kernel(dq_ref, iq_ref, iw_ref, iqn_ref, iwn_ref, dk_ref, vt_ref, ika_ref, ikb_ref,
                beta_ref, o_ref,
                key0_ref, key1_ref, plane0_ref, plane1_ref, eq_ref, sta_ref, stb_ref, lt_ref,
                qt_ref, acc_ref, m_ref, l_ref, *, topk, nblk, nq):
    j = pl.program_id(1)

    def npairs(i):
        return ((i + 2) // 2 + 1) // 2

    @pl.when(jnp.logical_and(pl.program_id(0) == 0, j == 0))
    def _():
        plane0_ref[...] = jnp.zeros(plane0_ref.shape, U32)
        plane1_ref[...] = jnp.zeros(plane1_ref.shape, U32)

    krow = lax.broadcasted_iota(I32, (KEY_CHUNK, Q_BLOCK), 0)
    qlane = lax.broadcasted_iota(I32, (KEY_CHUNK, Q_BLOCK), 1)

    def prep_scores(iq_blk, iw_blk):
        iq = iq_blk.astype(F32)
        for p in range(IDX_HEADS // 2):
            lt_ref[:, p * LANES:(p + 1) * LANES] = iq[:, p * LANES:(p + 1) * LANES].T.astype(BF16)
        iwt = iw_blk.astype(F32).T
        w_even = jnp.concatenate([iwt[2 * p:2 * p + 1, :] for p in range(IDX_HEADS // 2)], axis=1)
        w_odd = jnp.concatenate([iwt[2 * p + 1:2 * p + 2, :] for p in range(IDX_HEADS // 2)],
                                axis=1)
        return w_even, w_odd

    def score_chunk(c, q0, w_even, w_odd, key_ref, plane_ref, masked=True):
        k0 = pl.multiple_of(c * KEY_CHUNK, KEY_CHUNK)
        ka = ika_ref[pl.ds(k0, KEY_CHUNK), :]
        kb = ikb_ref[pl.ds(k0, KEY_CHUNK), :]
        sc = jnp.zeros((KEY_CHUNK, Q_BLOCK), F32)
        hw = P_TILE // 2
        for n in range(2):
            lt = lt_ref[:, n * hw:(n + 1) * hw]
            te = jnp.maximum(jnp.dot(ka, lt, preferred_element_type=F32), 0.0) * w_even[:, n * hw:(n + 1) * hw]
            to = jnp.maximum(jnp.dot(kb, lt, preferred_element_type=F32), 0.0) * w_odd[:, n * hw:(n + 1) * hw]
            for p in range(hw // LANES):
                sc = sc + te[:, p * LANES:(p + 1) * LANES]
                sc = sc + to[:, p * LANES:(p + 1) * LANES]
        key = _sortable_key(sc)
        if masked:
            key = jnp.where(krow + k0 <= qlane + q0, key, INT_MIN)
        key_ref[pl.ds(k0, KEY_CHUNK), :] = key
        u = pltpu.bitcast(key ^ INT_MIN, U32)
        planes = _bit_transpose32([u[8 * r:8 * r + 8, :] for r in range(32)])
        for b in range(32):
            plane_ref[b, c] = planes[b]

    def search(key_ref, plane_ref, npair):
        blk_id = lax.broadcasted_iota(I32, (nblk, 8, Q_BLOCK), 0)
        eq_ref[...] = jnp.where(blk_id < 2 * npair, jnp.uint32(0xFFFFFFFF), jnp.uint32(0))

        def make_pass(nb):
            def search_pass(k, carry):
                t_u, cnt_gt = carry
                b = 31 - k
                eq = eq_ref[0:nb]
                t = eq & plane_ref[b, 0:nb]
                pc = lax.population_count(t)
                tot = jnp.sum(_tree_sum([pc[r] for r in range(nb)]).astype(I32), axis=0,
                              keepdims=True)
                tot = jnp.broadcast_to(tot, (8, Q_BLOCK))
                take = cnt_gt + tot >= topk
                keep_eq = jnp.where(take, jnp.uint32(0), jnp.uint32(0xFFFFFFFF))
                eq_ref[0:nb] = (eq & keep_eq[None]) ^ t
                bit = jnp.left_shift(jnp.int32(1), b)
                return (jnp.where(take, t_u | bit, t_u), jnp.where(take, cnt_gt, cnt_gt + tot))
            return search_pass

        zero = (jnp.zeros((8, Q_BLOCK), I32), jnp.zeros((8, Q_BLOCK), I32))
        t_u, cnt_gt = lax.cond(
            2 * npair <= nblk // 2,
            lambda: lax.fori_loop(0, 32, make_pass(nblk // 2), zero),
            lambda: lax.fori_loop(0, 32, make_pass(nblk), zero))
        thr = jnp.maximum(t_u[0:1, :] ^ INT_MIN, INT_MIN + 1)

        pc_eq = lax.population_count(eq_ref[...])
        n_eq = jnp.sum(_tree_sum([pc_eq[r] for r in range(nblk)]).astype(I32), axis=0,
                       keepdims=True)
        excess = cnt_gt[0:1, :] + n_eq - topk

        @pl.when(jnp.max(excess) > 0)
        def _():
            keep = jnp.where(excess > 0, topk - cnt_gt[0:1, :], jnp.int32(2 ** 30))

            def count_tied_below(q_cut):
                def step(c, cnt):
                    k0 = pl.multiple_of(c * KEY_CHUNK, KEY_CHUNK)
                    kk = key_ref[pl.ds(k0, KEY_CHUNK), :]
                    hit = jnp.where(kk == thr, jnp.where(krow + k0 < q_cut, 1, 0), 0)
                    return cnt + jnp.sum(hit, axis=0, keepdims=True)
                return lax.fori_loop(0, 2 * npair, step, jnp.zeros((1, Q_BLOCK), I32))

            def pos_pass(k, q_max):
                cand = q_max | jnp.left_shift(jnp.int32(1), POS_BITS - 1 - k)
                return jnp.where(count_tied_below(cand) < keep, cand, q_max)

            q_max = lax.fori_loop(0, POS_BITS, pos_pass, jnp.zeros((1, Q_BLOCK), I32))

            def drop_step(c, carry):
                k0 = pl.multiple_of(c * KEY_CHUNK, KEY_CHUNK)
                kk = key_ref[pl.ds(k0, KEY_CHUNK), :]
                drop = jnp.where(kk == thr, jnp.where(krow + k0 > q_max, 1, 0), 0)
                key_ref[pl.ds(k0, KEY_CHUNK), :] = jnp.where(drop > 0, INT_MIN, kk)
                return carry

            lax.fori_loop(0, 2 * npair, drop_step, 0)

        return thr

    ones_rows = jnp.ones((16, KEY_CHUNK), BF16)

    def qk_dots(c, st_ref):
        k0 = pl.multiple_of(c * KEY_CHUNK, KEY_CHUNK)
        for g in range(DSA_KV_HEADS):
            kch = dk_ref[pl.ds(k0, KEY_CHUNK), g * LANES:(g + 1) * LANES]
            qt = qt_ref[:, g * DSA_REP * LANES:(g + 1) * DSA_REP * LANES]
            st_ref[g] = jnp.dot(kch, qt, preferred_element_type=F32)

    def softmax_pv(c, st_ref, key_ref, thr):
        k0 = pl.multiple_of(c * KEY_CHUNK, KEY_CHUNK)
        sel = key_ref[pl.ds(k0, KEY_CHUNK), :] >= thr
        for g in range(DSA_KV_HEADS):
            st = st_ref[g]
            st = jnp.concatenate(
                [jnp.where(sel, st[:, r * LANES:(r + 1) * LANES], NEG_BIG) for r in range(DSA_REP)],
                axis=1)
            m_old = m_ref[g:g + 1, :]
            m_new = jnp.maximum(m_old, jnp.max(st, axis=0, keepdims=True))
            alpha = jnp.exp2(m_old - m_new)
            p = jnp.exp2(st - m_new)
            m_ref[g:g + 1, :] = m_new
            vt = jnp.concatenate([vt_ref[0, c, g * LANES:(g + 1) * LANES, :], ones_rows], axis=0)
            pv = jnp.dot(vt, p.astype(BF16), preferred_element_type=F32)
            acc_ref[g] = acc_ref[g] * alpha + pv[:HEAD_DIM]
            l_ref[g:g + 1, :] = alpha * l_ref[g:g + 1, :] + pv[HEAD_DIM:HEAD_DIM + 1]

    def half_step(i_att, key_a, plane_a, row0, iq_blk, iw_blk, q0_sc, ps, key_s, plane_s):
        pa = npairs(i_att)
        thr = search(key_a, plane_a, pa)

        dq = dq_ref[row0:row0 + Q_BLOCK, :].astype(F32)
        for hh in range(DSA_HEADS):
            qt_ref[:, hh * LANES:(hh + 1) * LANES] = (
                dq[:, hh * LANES:(hh + 1) * LANES].T.astype(BF16))
        w_even, w_odd = prep_scores(iq_blk, iw_blk)
        acc_ref[...] = jnp.zeros(acc_ref.shape, F32)
        m_ref[...] = jnp.full(m_ref.shape, NEG_BIG, F32)
        l_ref[...] = jnp.zeros(l_ref.shape, F32)

        def score_pair(p, masked=True):
            score_chunk(2 * p, q0_sc, w_even, w_odd, key_s, plane_s, masked)
            score_chunk(2 * p + 1, q0_sc, w_even, w_odd, key_s, plane_s, masked)

        def fused_body(p, carry):
            c = 2 * p
            qk_dots(c + 1, stb_ref)
            softmax_pv(c, sta_ref, key_a, thr)
            qk_dots(c + 2, sta_ref)
            softmax_pv(c + 1, stb_ref, key_a, thr)
            score_pair(p, masked=False)
            return carry

        qk_dots(0, sta_ref)
        lax.fori_loop(0, pa - 1, fused_body, 0)
        c_last = 2 * (pa - 1)
        qk_dots(c_last + 1, stb_ref)
        softmax_pv(c_last, sta_ref, key_a, thr)
        softmax_pv(c_last + 1, stb_ref, key_a, thr)
        score_pair(pa - 1)

        @pl.when(ps > pa)
        def _():
            score_pair(pa)

        beta = beta_ref[...]
        for g in range(DSA_KV_HEADS):
            ot = acc_ref[g] / l_ref[g:g + 1, :]
            for r in range(DSA_REP):
                hh = g * DSA_REP + r
                blk = ot[:, r * LANES:(r + 1) * LANES].T
                o_ref[row0:row0 + Q_BLOCK, hh * LANES:(hh + 1) * LANES] = (
                    blk * beta[:, hh * LANES:(hh + 1) * LANES]).astype(BF16)

    i0 = 2 * j
    i1 = i0 + 1
    i2 = i0 + 2

    @pl.when(j == 0)
    def _():
        w_even, w_odd = prep_scores(iq_ref[0:Q_BLOCK, :], iw_ref[0:Q_BLOCK, :])
        score_chunk(0, 0, w_even, w_odd, key0_ref, plane0_ref)
        score_chunk(1, 0, w_even, w_odd, key0_ref, plane0_ref)

    half_step(i0, key0_ref, plane0_ref, 0,
              iq_ref[Q_BLOCK:2 * Q_BLOCK, :], iw_ref[Q_BLOCK:2 * Q_BLOCK, :], i1 * Q_BLOCK,
              npairs(i1), key1_ref, plane1_ref)
    has_next = i2 < nq
    half_step(i1, key1_ref, plane1_ref, Q_BLOCK,
              iqn_ref[...], iwn_ref[...], jnp.minimum(i2, nq - 1) * Q_BLOCK,
              jnp.where(has_next, npairs(i2), npairs(i1)), key0_ref, plane0_ref)


def _dsa_call(P, vT, beta, *, B, L, topk):
    T = B * L
    nq = L // Q_BLOCK
    nq2 = nq // 2
    nblk = L // KEY_CHUNK
    kern = functools.partial(_dsa_kernel, topk=topk, nblk=nblk, nq=nq)
    two = 2 * Q_BLOCK
    qrow2 = lambda col, width: pl.BlockSpec((two, width),
                                            lambda b, j, col=col: (b * nq2 + j, col))
    nxt = lambda col, width: pl.BlockSpec(
        (Q_BLOCK, width), lambda b, j, col=col: (b * nq + jnp.minimum(2 * j + 2, nq - 1), col))
    tail128 = TAIL_COL // LANES
    plane = pltpu.VMEM((32, nblk, 8, Q_BLOCK), U32)
    return pl.pallas_call(
        kern,
        grid=(B, nq2),
        in_specs=[
            qrow2(4, P_TILE),
            qrow2(5, P_TILE),
            qrow2(tail128 + 6, LANES),
            nxt(5, P_TILE),
            nxt(tail128 + 6, LANES),
            pl.BlockSpec((L, DSA_KV_W), lambda b, j: (b, TAIL_COL // DSA_KV_W)),
            pl.BlockSpec((1, L // KEY_CHUNK, KEY_CHUNK, KEY_CHUNK), lambda b, j: (b, 0, 0, 0)),
            pl.BlockSpec((L, LANES), lambda b, j: (b, tail128 + 4)),
            pl.BlockSpec((L, LANES), lambda b, j: (b, tail128 + 5)),
            pl.BlockSpec((1, DSA_W), lambda b, j: (0, 0)),
        ],
        out_specs=pl.BlockSpec((two, DSA_W), lambda b, j: (b * nq2 + j, 0)),
        out_shape=jax.ShapeDtypeStruct((T, DSA_W), BF16),
        scratch_shapes=[
            pltpu.VMEM((L, Q_BLOCK), I32), pltpu.VMEM((L, Q_BLOCK), I32),
            plane, plane,
            pltpu.VMEM((nblk, 8, Q_BLOCK), U32),
            pltpu.VMEM((DSA_KV_HEADS, KEY_CHUNK, DSA_REP * Q_BLOCK), F32),
            pltpu.VMEM((DSA_KV_HEADS, KEY_CHUNK, DSA_REP * Q_BLOCK), F32),
            pltpu.VMEM((LANES, P_TILE), BF16),
            pltpu.VMEM((HEAD_DIM, DSA_W), BF16),
            pltpu.VMEM((DSA_KV_HEADS, HEAD_DIM, DSA_REP * Q_BLOCK), F32),
            pltpu.VMEM((DSA_KV_HEADS, DSA_REP * Q_BLOCK), F32),
            pltpu.VMEM((DSA_KV_HEADS, DSA_REP * Q_BLOCK), F32),
        ],
        compiler_params=_cparams(("arbitrary", "arbitrary")),
        name="dsa",
    )(P, P, P, P, P, P, vT, P, P, beta)


def _outproj_kernel(yr_ref, yd_ref, x_ref, mod_ref, g2_ref, w_ref, o_ref, h_ref, *, tm):
    half = tm // 2
    for r in range(2):
        rows = slice(r * half, (r + 1) * half)
        mix = (jnp.dot(yr_ref[rows, :], w_ref[0:RET_W, :], preferred_element_type=F32)
               + jnp.dot(yd_ref[rows, :], w_ref[RET_W:RET_W + DSA_W, :],
                         preferred_element_type=F32))
        x1 = x_ref[rows, :] + mod_ref[0, 2:3, :] * mix
        o_ref[rows, :] = x1
        ms = jnp.mean(x1 * x1, axis=-1, keepdims=True)
        y = (x1 * lax.rsqrt(ms + EPS)) * g2_ref[...]
        h_ref[rows, :] = (y * (1.0 + mod_ref[0, 4:5, :]) + mod_ref[0, 3:4, :]).astype(BF16)


def _outproj_call(y_ret, y_dsa, x2, mod3, g2, w_out, *, B, L, tm):
    T = B * L
    tpb = L // tm
    row = lambda w: pl.BlockSpec((tm, w), lambda i: (i, 0))
    return pl.pallas_call(
        functools.partial(_outproj_kernel, tm=tm),
        grid=(T // tm,),
        in_specs=[
            row(RET_W), row(DSA_W), row(D_MODEL),
            pl.BlockSpec((1, N_MOD, D_MODEL), lambda i: (i // tpb, 0, 0)),
            pl.BlockSpec((1, D_MODEL), lambda i: (0, 0)),
            pl.BlockSpec((RET_W + DSA_W, D_MODEL), lambda i: (0, 0)),
        ],
        out_specs=[row(D_MODEL), row(D_MODEL)],
        out_shape=[jax.ShapeDtypeStruct((T, D_MODEL), F32),
                   jax.ShapeDtypeStruct((T, D_MODEL), BF16)],
        compiler_params=_cparams(("arbitrary",)),
        name="outproj",
    )(y_ret, y_dsa, x2, mod3, g2, w_out)


def _mlp_kernel(h_ref, xs_ref, gate_ref, w1_ref, w2_ref, o_ref, a_ref, *, tm):
    s = pl.program_id(1)

    @pl.when(s < MLP_NA)
    def _():
        hid = jnp.dot(h_ref[...], w1_ref[...], preferred_element_type=F32)
        a = jnp.maximum(hid, 0.0)
        a_ref[s] = (a * a).astype(BF16)

    @pl.when(s >= MLP_NA)
    def _():
        half = tm // 2
        for r in range(2):
            rows = slice(r * half, (r + 1) * half)
            y = None
            for f in range(MLP_NA):
                d = jnp.dot(a_ref[f, rows, :], w2_ref[f * MLP_TF:(f + 1) * MLP_TF, :],
                            preferred_element_type=F32)
                y = d if y is None else y + d
            o_ref[rows, :] = xs_ref[rows, :] + gate_ref[0] * y


def _mlp_call(h2, x1, mod3, w1, w2, *, B, L, tm):
    T = B * L
    tpb = L // tm
    gate2 = mod3[:, 5:6, :]
    kern = functools.partial(_mlp_kernel, tm=tm)
    ncol = lambda s: jnp.maximum(s - MLP_NA, 0)
    return pl.pallas_call(
        kern,
        grid=(T // tm, MLP_NA + MLP_NB),
        in_specs=[
            pl.BlockSpec((tm, D_MODEL), lambda i, s: (i, 0)),
            pl.BlockSpec((tm, MLP_TN), lambda i, s: (i, ncol(s))),
            pl.BlockSpec((1, 1, MLP_TN), lambda i, s: (i // tpb, 0, ncol(s))),
            pl.BlockSpec((D_MODEL, MLP_TF), lambda i, s: (0, jnp.minimum(s, MLP_NA - 1))),
            pl.BlockSpec((D_FF, MLP_TN), lambda i, s: (0, ncol(s))),
        ],
        out_specs=pl.BlockSpec((tm, MLP_TN), lambda i, s: (i, ncol(s))),
        out_shape=jax.ShapeDtypeStruct((T, D_MODEL), F32),
        scratch_shapes=[pltpu.VMEM((MLP_NA, tm, MLP_TF), BF16)],
        compiler_params=_cparams(("arbitrary", "arbitrary")),
        name="mlp",
    )(h2, x1, gate2, w1, w2)


def _deinterleave_heads(w):
    k = w.shape[0]
    return w.reshape(k, RET_HEADS, HEAD_DIM // 2, 2).transpose(0, 1, 3, 2).reshape(k, RET_W)


def _prep_w_in(w):
    o = np.cumsum([0, RET_W, RET_W, RET_W, RET_W, DSA_W, DSA_KV_W, DSA_KV_W,
                   IDX_HEADS * IDX_DIM, IDX_DIM, IDX_HEADS])
    part = lambda n: w[:, o[n]:o[n + 1]]
    z = lambda n: jnp.zeros((w.shape[0], n), w.dtype)
    cols = [_deinterleave_heads(part(0)), _deinterleave_heads(part(1)), part(2), part(3),
            part(4), part(7),
            part(5), part(6), part(8), z(LANES - IDX_DIM), z(LANES),
            part(9), z(LANES - IDX_HEADS), z(LANES)]
    return jnp.concatenate(cols, axis=1).astype(BF16)


def _rot_tables(L):
    pos = np.arange(L, dtype=np.float32)
    angle = (np.float32(1.0) / np.float32(10000.0) ** np.linspace(0.0, 1.0, HEAD_DIM // 2,
                                                                  dtype=np.float32)).astype(np.float32)
    theta = (pos[:, None] * angle[None, :]).astype(np.float32)
    s, c = np.sin(theta).astype(np.float32), np.cos(theta).astype(np.float32)
    return np.concatenate([c, c], axis=1), np.concatenate([-s, s], axis=1)


def _ret_tables():
    log_gamma = _ret_consts()
    C = RET_BLOCK
    pos = np.arange(C, dtype=np.float32)
    diff = pos[:, None] - pos[None, :]
    dmask = np.where(diff[None] >= 0,
                     np.exp(log_gamma[:, None, None] * np.maximum(diff, 0.0)[None]), 0.0)
    xi = np.exp(log_gamma[:, None] * (pos[None, :] + np.float32(1.0)))
    zeta = np.exp(log_gamma[:, None] * (np.float32(C - 1.0) - pos[None, :]))
    bc = lambda a: np.ascontiguousarray(
        np.broadcast_to(a.astype(np.float32)[:, :, None], (RET_HEADS, C, LANES)))
    return dmask.astype(np.float32), bc(xi), bc(zeta)


def _pad_lanes(v, n):
    return jnp.concatenate([v, jnp.zeros((n - v.shape[0],), v.dtype)]).reshape(1, n)


def _layer(x, c, w_ada, b_ada, norm1_g, norm2_g, w_in, ret_beta, q_norm_g, k_norm_g,
           idx_k_norm_g, dsa_beta, w_out, w_mlp1, w_mlp2):
    B, L, _ = x.shape
    assert L % (2 * KEY_CHUNK) == 0 and L <= 2 ** (POS_BITS - 1)
    topk = min(TOPK_MAX, L // 4)
    tm = min(1024, L)
    x2 = x.reshape(B * L, D_MODEL)

    mod3 = _mod_call(c, w_ada, b_ada).reshape(B, N_MOD, D_MODEL)

    cos_t, sin_t = _rot_tables(L)
    P, vT = _inproj_call(x2, mod3, norm1_g.reshape(1, -1), _prep_w_in(w_in), cos_t, sin_t,
                         q_norm_g.reshape(1, -1), k_norm_g.reshape(1, -1),
                         _pad_lanes(idx_k_norm_g, LANES), B=B, L=L, tm=tm)

    dmask, xi_b, zeta_b = _ret_tables()
    decay = tuple(float(v) for v in np.exp(_ret_consts() * np.float32(RET_BLOCK)))
    y_ret = _retention_call(P, dmask, xi_b, zeta_b, ret_beta.reshape(1, -1), decay,
                            B=B, L=L, rt=tm)
    y_dsa = _dsa_call(P, vT, dsa_beta.reshape(1, -1), B=B, L=L, topk=topk)

    x1, h2 = _outproj_call(y_ret, y_dsa, x2, mod3, norm2_g.reshape(1, -1), w_out.astype(BF16),
                           B=B, L=L, tm=min(512, L))
    out = _mlp_call(h2, x1, mod3, w_mlp1.astype(BF16), w_mlp2.astype(BF16), B=B, L=L, tm=tm)
    return out.reshape(B, L, D_MODEL)


def kernel(x, c, w_ada, b_ada, norm1_g, norm2_g, w_in, ret_beta, q_norm_g, k_norm_g,
           idx_k_norm_g, dsa_beta, w_out, w_mlp1, w_mlp2):
    for l in range(w_ada.shape[0]):
        x = _layer(x, c, w_ada[l], b_ada[l], norm1_g[l], norm2_g[l], w_in[l], ret_beta[l],
                   q_norm_g[l], k_norm_g[l], idx_k_norm_g[l], dsa_beta[l], w_out[l],
                   w_mlp1[l], w_mlp2[l])
    return x
```
